```python
import jax, jax.numpy as jnp
from jax import lax
import numpy as np

D_MODEL = 1024
BATCH = 4
SEQ = 4096
DEPTH = 1
DEC_BATCH = 128
DEC_SEQ = 4
PAST_LEN = 2048
PAGE_SIZE = 128

HEAD_DIM = 64
N_HEADS = 12
ATTN_WIDTH = N_HEADS * HEAD_DIM
CONV_CH = D_MODEL - ATTN_WIDTH
CONV_GROUPS = CONV_CH // HEAD_DIM
CONV_WIDTH = 31
IN_COLS = 3 * ATTN_WIDTH + 2 * CONV_CH
DILATIONS = ((128, 1), (512, 4), (2048, 16))
MAX_WINDOW = 2048
BAND_BLOCK = 128
ROPE_THETA = 10000.0
ATTN_SCALE = HEAD_DIM ** -0.5
NEG_INF = -1e30
N_KEYS = 128
N_EXPERTS = N_KEYS * N_KEYS
PEER_HEADS = 8
PEER_TOPK = 16
D_QUERY = 256
HALF_Q = D_QUERY // 2
PEER_CHUNK = 256
RMS_EPS = 1e-6
LN_EPS = 1e-5

kernel_name = 'hymba_conformer_dilated_peer_step'


def rms_norm(x, g):
    xf = x.astype(jnp.float32)
    y = xf * lax.rsqrt(jnp.mean(xf * xf, axis=-1, keepdims=True) + RMS_EPS)
    return (y * g.astype(jnp.float32)).astype(x.dtype)


def rope(x, pos):
    half = HEAD_DIM // 2
    inv = jnp.power(ROPE_THETA, -jnp.arange(half, dtype=jnp.float32) * (2.0 / HEAD_DIM))
    ang = pos.astype(jnp.float32)[:, None] * inv[None, :]
    cos = jnp.cos(ang)[None, :, None, :]
    sin = jnp.sin(ang)[None, :, None, :]
    xf = x.astype(jnp.float32)
    x1, x2 = xf[..., :half], xf[..., half:]
    return jnp.concatenate([x1 * cos - x2 * sin, x2 * cos + x1 * sin], axis=-1).astype(x.dtype)


def in_projection(h, w_in):
    p = h @ w_in
    lead = p.shape[:-1]
    q = p[..., :ATTN_WIDTH].reshape(lead + (N_HEADS, HEAD_DIM))
    k = p[..., ATTN_WIDTH:2 * ATTN_WIDTH].reshape(lead + (N_HEADS, HEAD_DIM))
    v = p[..., 2 * ATTN_WIDTH:3 * ATTN_WIDTH].reshape(lead + (N_HEADS, HEAD_DIM))
    u = p[..., 3 * ATTN_WIDTH:]
    g = u[..., :CONV_CH] * jax.nn.sigmoid(u[..., CONV_CH:])
    return q, k, v, g


def banded_branch(q, k, v, dil, reach):
    b, s, h, dh = q.shape
    L = s // dil
    n = b * dil
    Q = BAND_BLOCK
    nb = -(-L // Q)
    lp = nb * Q

    def to_sub(t):
        return t.reshape(b, L, dil, h, dh).transpose(0, 2, 1, 3, 4).reshape(n, L, h, dh)

    qs = jnp.pad(to_sub(q), ((0, 0), (0, lp - L), (0, 0), (0, 0))).reshape(n, nb, Q, h, dh)

    def key_blocks(t):
        t = jnp.pad(to_sub(t), ((0, 0), (Q, lp - L), (0, 0), (0, 0))).reshape(n, nb + 1, Q, h, dh)
        return jnp.concatenate([t[:, :-1], t[:, 1:]], axis=2)

    kb = key_blocks(k)
    vb = key_blocks(v)
    qi = jnp.arange(Q)[:, None]
    ki = jnp.arange(2 * Q)[None, :]
    dist = qi + Q - ki
    blk = jnp.arange(nb)[:, None, None]
    valid = (dist >= 0) & (dist <= reach) & (blk * Q - Q + ki >= 0)
    sc = jnp.einsum('nbqhd,nbkhd->nbhqk', qs, kb, preferred_element_type=jnp.float32) * ATTN_SCALE
    sc = jnp.where(valid[None, :, None], sc, NEG_INF)
    m = sc.max(axis=-1)
    p = jnp.exp(sc - m[..., None])
    l = p.sum(axis=-1)
    o = jnp.einsum('nbhqk,nbkhd->nbqhd', p, vb.astype(jnp.float32))

    def from_sub(t):
        t = t.reshape((n, lp) + t.shape[3:])[:, :L]
        t = jnp.swapaxes(t.reshape((b, dil, L) + t.shape[2:]), 1, 2)
        return t.reshape((b, s) + t.shape[3:])

    return from_sub(o), from_sub(jnp.swapaxes(m, 2, 3)), from_sub(jnp.swapaxes(l, 2, 3))


def gathered_branch(q, k_all, v_all, offset, dil, reach):
    t = q.shape[1]
    idx = offset + jnp.arange(t)[:, None] - dil * jnp.arange(reach + 1)[None, :]
    valid = idx >= 0
    idx = jnp.maximum(idx, 0)
    kg = k_all[:, idx]
    vg = v_all[:, idx]
    sc = jnp.einsum('bthd,btrhd->bthr', q, kg, preferred_element_type=jnp.float32) * ATTN_SCALE
    sc = jnp.where(valid[None, :, None, :], sc, NEG_INF)
    m = sc.max(axis=-1)
    p = jnp.exp(sc - m[..., None])
    l = p.sum(axis=-1)
    o = jnp.einsum('bthr,btrhd->bthd', p, vg.astype(jnp.float32))
    return o, m, l


def merge_branches(parts):
    m_all = jnp.stack([pm for _, pm, _ in parts])
    l_all = jnp.stack([pl for _, _, pl in parts])
    o_all = jnp.stack([po for po, _, _ in parts])
    w = jnp.exp(m_all - m_all.max(axis=0))
    den = (w * l_all).sum(axis=0)
    out = (w[..., None] * o_all).sum(axis=0) / den[..., None]
    return out.reshape(out.shape[:2] + (ATTN_WIDTH,))


def conformer_conv(g_hist, conv_w, conv_b, ln_g, ln_b):
    y = lax.conv_general_dilated(g_hist, conv_w[:, None, :].astype(g_hist.dtype), (1,), 'VALID',
                                 dimension_numbers=('NWC', 'WIO', 'NWC'),
                                 feature_group_count=CONV_CH)
    yf = (y + conv_b).astype(jnp.float32)
    mu = jnp.mean(yf, axis=-1, keepdims=True)
    var = jnp.mean(jnp.square(yf - mu), axis=-1, keepdims=True)
    yn = (yf - mu) * lax.rsqrt(var + LN_EPS) * ln_g.astype(jnp.float32) + ln_b.astype(jnp.float32)
    return (yn * jax.nn.sigmoid(yn)).astype(g_hist.dtype)


def peer(x, w_query, keys1, keys2, exp_u, exp_v):
    T, D = x.shape
    nc = -(-T // PEER_CHUNK)
    xs = jnp.pad(x, ((0, nc * PEER_CHUNK - T), (0, 0))).reshape(nc, PEER_CHUNK, D)
    k1 = keys1.astype(jnp.float32)
    k2 = keys2.astype(jnp.float32)

    def chunk_fn(xc):
        q = (xc @ w_query).astype(jnp.float32).reshape(PEER_CHUNK, PEER_HEADS, D_QUERY)
        s1 = jnp.einsum('chk,nk->chn', q[..., :HALF_Q], k1)
        s2 = jnp.einsum('chk,nk->chn', q[..., HALF_Q:], k2)
        v1, i1 = lax.top_k(s1, PEER_TOPK)
        v2, i2 = lax.top_k(s2, PEER_TOPK)
        cand = (v1[..., :, None] + v2[..., None, :]).reshape(PEER_CHUNK, PEER_HEADS, PEER_TOPK * PEER_TOPK)
        sc, ci = lax.top_k(cand, PEER_TOPK)
        e1 = jnp.take_along_axis(i1, ci // PEER_TOPK, axis=-1)
        e2 = jnp.take_along_axis(i2, ci % PEER_TOPK, axis=-1)
        eid = e1 * N_KEYS + e2
        gate = jax.nn.softmax(sc, axis=-1)
        act = jax.nn.gelu(jnp.einsum('chkd,cd->chk', exp_u[eid], xc, preferred_element_type=jnp.float32))
        out = jnp.einsum('chk,chkd->cd', gate * act, exp_v[eid].astype(jnp.float32))
        return out.astype(xc.dtype)

    return lax.map(chunk_fn, xs).reshape(nc * PEER_CHUNK, D)[:T]


def channel_block(x, g, w_query, keys1, keys2, exp_u, exp_v):
    h = rms_norm(x, g)
    y = peer(h.reshape(-1, D_MODEL), w_query, keys1, keys2, exp_u, exp_v)
    return x + y.reshape(x.shape)


def setup_inputs(seed: int = 0) -> dict:
    key = jax.random.key(seed)
    ks = jax.random.split(key, 20)
    w_buf = min(MAX_WINDOW, PAST_LEN)

    def nrm(k, shape, s):
        return s * jax.random.normal(k, shape, jnp.float32)

    return {
        'x_prompt': nrm(ks[0], (BATCH, SEQ, D_MODEL), 1.0),
        'x_sample': nrm(ks[1], (DEC_BATCH, DEC_SEQ, D_MODEL), 1.0),
        'cache_k': nrm(ks[2], (DEPTH, DEC_BATCH, w_buf, N_HEADS, HEAD_DIM), 1.0),
        'cache_v': nrm(ks[3], (DEPTH, DEC_BATCH, w_buf, N_HEADS, HEAD_DIM), 1.0),
        'state_conv': nrm(ks[4], (DEPTH, DEC_BATCH, CONV_WIDTH - 1, CONV_CH), 0.5),
        'norm1_g': 1.0 + nrm(ks[5], (DEPTH, D_MODEL), 0.01),
        'w_in': nrm(ks[6], (DEPTH, D_MODEL, IN_COLS), D_MODEL ** -0.5),
        'conv_w': nrm(ks[7], (DEPTH, CONV_WIDTH, CONV_CH), CONV_WIDTH ** -0.5),
        'conv_b': nrm(ks[8], (DEPTH, CONV_CH), 0.01),
        'conv_ln_g': 1.0 + nrm(ks[9], (DEPTH, CONV_CH), 0.01),
        'conv_ln_b': nrm(ks[10], (DEPTH, CONV_CH), 0.01),
        'w_out': nrm(ks[11], (DEPTH, D_MODEL, D_MODEL), D_MODEL ** -0.5),
        'norm2_g': 1.0 + nrm(ks[12], (DEPTH, D_MODEL), 0.01),
        'w_query': nrm(ks[13], (DEPTH, D_MODEL, PEER_HEADS * D_QUERY), D_MODEL ** -0.5),
        'sub_keys1': nrm(ks[14], (DEPTH, N_KEYS, HALF_Q), HALF_Q ** -0.5),
        'sub_keys2': nrm(ks[15], (DEPTH, N_KEYS, HALF_Q), HALF_Q ** -0.5),
        'expert_u': nrm(ks[16], (DEPTH, N_EXPERTS, D_MODEL), D_MODEL ** -0.5),
        'expert_v': nrm(ks[17], (DEPTH, N_EXPERTS, D_MODEL), 0.5),
        'final_g': 1.0 + nrm(ks[18], (D_MODEL,), 0.01),
    }


def reference(x_prompt, x_sample, cache_k, cache_v, state_conv, norm1_g, w_in, conv_w, conv_b,
              conv_ln_g, conv_ln_b, w_out, norm2_g, w_query, sub_keys1, sub_keys2,
              expert_u, expert_v, final_g):
    bp, s = x_prompt.shape[:2]
    t = x_sample.shape[1]
    w_buf = cache_k.shape[2]
    keep = min(MAX_WINDOW, s)
    pos_p = jnp.arange(s)
    pos_s = PAST_LEN + jnp.arange(t)
    yp, ys = x_prompt, x_sample
    nkp, nvp, ncp, nks, nvs, ncs = [], [], [], [], [], []
    for l in range(DEPTH):
        q, k, v, g = in_projection(rms_norm(yp, norm1_g[l]), w_in[l])
        q = rope(q, pos_p)
        k = rope(k, pos_p)
        attn = merge_branches([banded_branch(q, k, v, dil, win // dil) for win, dil in DILATIONS])
        g_hist = jnp.concatenate([jnp.zeros((bp, CONV_WIDTH - 1, CONV_CH), g.dtype), g], axis=1)
        c = conformer_conv(g_hist, conv_w[l], conv_b[l], conv_ln_g[l], conv_ln_b[l])
        yp = yp + jnp.concatenate([attn.astype(yp.dtype), c], axis=-1) @ w_out[l]
        yp = channel_block(yp, norm2_g[l], w_query[l], sub_keys1[l], sub_keys2[l], expert_u[l], expert_v[l])
        nkp.append(k[:, s - keep:])
        nvp.append(v[:, s - keep:])
        ncp.append(g_hist[:, -(CONV_WIDTH - 1):])
        q, k, v, g = in_projection(rms_norm(ys, norm1_g[l]), w_in[l])
        q = rope(q, pos_s)
        k = rope(k, pos_s)
        k_all = jnp.concatenate([cache_k[l], k], axis=1)
        v_all = jnp.concatenate([cache_v[l], v], axis=1)
        attn = merge_branches([gathered_branch(q, k_all, v_all, w_buf, dil, win // dil)
                               for win, dil in DILATIONS])
        g_hist = jnp.concatenate([state_conv[l], g], axis=1)
        c = conformer_conv(g_hist, conv_w[l], conv_b[l], conv_ln_g[l], conv_ln_b[l])
        ys = ys + jnp.concatenate([attn.astype(ys.dtype), c], axis=-1) @ w_out[l]
        ys = channel_block(ys, norm2_g[l], w_query[l], sub_keys1[l], sub_keys2[l], expert_u[l], expert_v[l])
        nks.append(k)
        nvs.append(v)
        ncs.append(g_hist[:, -(CONV_WIDTH - 1):])
    y_prompt = rms_norm(yp, final_g)
    y_sample = rms_norm(ys, final_g)
    return (y_prompt, y_sample, jnp.stack(nkp), jnp.stack(nvp), jnp.stack(ncp),
            jnp.stack(nks), jnp.stack(nvs), jnp.stack(ncs))
```

```python
import functools

import jax
import jax.numpy as jnp
from jax import lax
from jax.experimental import pallas as pl
from jax.experimental.pallas import tpu as pltpu

F32 = jnp.float32
BF16 = jnp.bfloat16

D_MODEL = 1024
HEAD_DIM = 64
HALF_HEAD = HEAD_DIM // 2
N_HEADS = 12
ATTN_WIDTH = N_HEADS * HEAD_DIM
CONV_CH = D_MODEL - ATTN_WIDTH
CONV_WIDTH = 31
CONV_HIST = CONV_WIDTH - 1
IN_COLS = 3 * ATTN_WIDTH + 2 * CONV_CH
DILATIONS = ((128, 1), (512, 4), (2048, 16))
MAX_WINDOW = 2048
PAST_LEN = 2048
BAND_BLOCK = 128
ROPE_THETA = 10000.0
ATTN_SCALE = HEAD_DIM ** -0.5
NEG_INF = -1e30
N_KEYS = 128
N_EXPERTS = N_KEYS * N_KEYS
PEER_HEADS = 8
PEER_TOPK = 16
D_QUERY = 256
HALF_Q = D_QUERY // 2
RMS_EPS = 1e-6
LN_EPS = 1e-5

LANES = 128
HEADS_PER_LANE_TILE = LANES // HEAD_DIM
N_HEAD_TILES = ATTN_WIDTH // LANES
NOT_SELECTED = 99.0
VMEM_LIMIT = 56 * 1024 * 1024


def _cparams(semantics):
    return pltpu.CompilerParams(dimension_semantics=semantics, vmem_limit_bytes=VMEM_LIMIT)


def _rms(x, g):
    return (x * lax.rsqrt(jnp.mean(x * x, axis=-1, keepdims=True) + RMS_EPS)) * g


def _inproj_body(x_ref, g_ref, w_ref, cos_ref, sin_ref, q_ref, k_ref, v_ref, gl_ref):
    hb = _rms(x_ref[...], g_ref[...]).astype(BF16)
    cos = cos_ref[...]
    sin = sin_ref[...]
    lane = lax.broadcasted_iota(jnp.int32, cos.shape, 1)
    first_half = jnp.bitwise_and(lane, HEAD_DIM - 1) < HALF_HEAD

    def rope(t):
        partner = jnp.where(first_half, pltpu.roll(t, LANES - HALF_HEAD, 1), pltpu.roll(t, HALF_HEAD, 1))
        return t * cos + partner * sin

    q = jnp.dot(hb, w_ref[:, 0:ATTN_WIDTH], preferred_element_type=F32)
    k = jnp.dot(hb, w_ref[:, ATTN_WIDTH:2 * ATTN_WIDTH], preferred_element_type=F32)
    for c in range(N_HEAD_TILES):
        sl = slice(c * LANES, (c + 1) * LANES)
        q_ref[:, sl] = rope(q[:, sl]) * ATTN_SCALE
        k_ref[:, sl] = rope(k[:, sl])
    v_ref[...] = jnp.dot(hb, w_ref[:, 2 * ATTN_WIDTH:3 * ATTN_WIDTH], preferred_element_type=F32)
    u = jnp.dot(hb, w_ref[:, 3 * ATTN_WIDTH:IN_COLS], preferred_element_type=F32)
    gl_ref[...] = u[:, :CONV_CH] * jax.nn.sigmoid(u[:, CONV_CH:])


def _in_projection(x, g, w_bf16, cos, sin, tm):
    T = x.shape[0]
    tok = lambda i: (i, 0)
    full = lambda i: (0, 0)
    return pl.pallas_call(
        _inproj_body,
        grid=(T // tm,),
        in_specs=[pl.BlockSpec((tm, D_MODEL), tok), pl.BlockSpec((1, D_MODEL), full),
                  pl.BlockSpec((D_MODEL, IN_COLS), full),
                  pl.BlockSpec((tm, LANES), tok), pl.BlockSpec((tm, LANES), tok)],
        out_specs=[pl.BlockSpec((tm, ATTN_WIDTH), tok)] * 3 + [pl.BlockSpec((tm, CONV_CH), tok)],
        out_shape=[jax.ShapeDtypeStruct((T, ATTN_WIDTH), F32)] * 3 + [jax.ShapeDtypeStruct((T, CONV_CH), F32)],
        compiler_params=_cparams(("parallel",)),
        name="in_projection",
    )(x, g, w_bf16, cos, sin)


def _rope_tables(pos):
    inv = jnp.power(ROPE_THETA, -jnp.arange(HALF_HEAD, dtype=F32) * (2.0 / HEAD_DIM))
    ang = pos.astype(F32)[:, None] * inv[None, :]
    cos, sin = jnp.cos(ang), jnp.sin(ang)
    return (jnp.concatenate([cos, cos] * HEADS_PER_LANE_TILE, axis=-1),
            jnp.concatenate([-sin, sin] * HEADS_PER_LANE_TILE, axis=-1))


def _attn_prompt_body(q_ref, k_ref, v_ref, o_ref, m_ref, l_ref, acc_ref, *, seq):
    Q = BAND_BLOCK
    m_ref[...] = jnp.full(m_ref.shape, NEG_INF, F32)
    l_ref[...] = jnp.zeros(l_ref.shape, F32)
    acc_ref[...] = jnp.zeros(acc_ref.shape, F32)
    lane = lax.broadcasted_iota(jnp.int32, (Q, LANES), 1)
    head_lanes = [lane < HEAD_DIM, lane >= HEAD_DIM]
    qi =lax.broadcasted_iota(jnp.int32, (Q, 2 * Q), 0)
    ki = lax.broadcasted_iota(jnp.int32, (Q, 2 * Q), 1)

    for win, dil in DILATIONS:
        reach = win // dil
        n_blocks = (seq // dil) // Q

        def rows(start, size, dil=dil):
            return pl.ds(start, size) if dil == 1 else pl.ds(start, size, stride=dil)

        def step(idx, carry, dil=dil, reach=reach, n_blocks=n_blocks, rows=rows):
            r = idx // n_blocks
            a0 = (idx % n_blocks) * Q
            ks = jnp.maximum(a0 - Q, 0)
            q_rows = rows(r + a0 * dil, Q)
            k_rows = rows(r + ks * dil, 2 * Q)
            q = q_ref[q_rows, :]
            k = k_ref[k_rows, :].astype(BF16)
            v = v_ref[k_rows, :].astype(BF16)
            dist = (a0 - ks) + qi - ki
            valid = (dist >= 0) & (dist <= reach)
            acc_new = jnp.zeros((Q, LANES), F32)
            alpha_full = jnp.zeros((Q, LANES), F32)
            for h in range(HEADS_PER_LANE_TILE):
                qh = jnp.where(head_lanes[h], q, 0.0).astype(BF16)
                s = lax.dot_general(qh, k, (((1,), (1,)), ((), ())), preferred_element_type=F32)
                s = jnp.where(valid, s, NEG_INF)
                m_prev = m_ref[h, q_rows, :]
                l_prev = l_ref[h, q_rows, :]
                m_new = jnp.maximum(m_prev, jnp.max(s, axis=1, keepdims=True))
                alpha = jnp.exp(m_prev - m_new)
                p = jnp.exp(s - m_new[:, :1])
                m_ref[h, q_rows, :] = m_new
                l_ref[h, q_rows, :] = alpha * l_prev + jnp.sum(p, axis=1, keepdims=True)
                pv = jnp.dot(p.astype(BF16), v, preferred_element_type=F32)
                acc_new = jnp.where(head_lanes[h], pv, acc_new)
                alpha_full = jnp.where(head_lanes[h], alpha, alpha_full)
            acc_ref[q_rows, :] = acc_ref[q_rows, :] * alpha_full + acc_new
            return carry

        lax.fori_loop(0, dil * n_blocks, step, 0)

    lane_s = lax.broadcasted_iota(jnp.int32, (seq, LANES), 1)
    denom = jnp.where(lane_s < HEAD_DIM, l_ref[0], l_ref[1])
    o_ref[...] = (acc_ref[...] / denom).astype(o_ref.dtype)


def _attention_prompt(q, k, v, batch, seq):
    T = q.shape[0]
    blk = pl.BlockSpec((seq, LANES), lambda b, hp: (b, hp))
    return pl.pallas_call(
        functools.partial(_attn_prompt_body, seq=seq),
        grid=(batch, N_HEAD_TILES),
        in_specs=[blk, blk, blk],
        out_specs=blk,
        out_shape=jax.ShapeDtypeStruct((T, ATTN_WIDTH), BF16),
        scratch_shapes=[pltpu.VMEM((HEADS_PER_LANE_TILE, seq, LANES), F32),
                        pltpu.VMEM((HEADS_PER_LANE_TILE, seq, LANES), F32),
                        pltpu.VMEM((seq, LANES), F32)],
        compiler_params=_cparams(("parallel", "parallel")),
        name="attention_prompt",
    )(q, k, v)


SAMPLE_ROWS = 16


def _attn_sample_body(q_ref, kn_ref, vn_ref, kt_ref, vt_ref, ks_ref, vs_ref, o_ref, kc_ref, vc_ref,
                      *, t_new, tail, groups, past):
    n_tail_keys = tail + 16
    kc_ref[0:tail, :] = kt_ref[0].astype(BF16)
    vc_ref[0:tail, :] = vt_ref[0].astype(BF16)
    pad = jnp.zeros((16 - t_new, ATTN_WIDTH), F32)
    kc_ref[tail:n_tail_keys, :] = jnp.concatenate([kn_ref[0], pad], axis=0).astype(BF16)
    vc_ref[tail:n_tail_keys, :] = jnp.concatenate([vn_ref[0], pad], axis=0).astype(BF16)

    n_rows = t_new * SAMPLE_ROWS
    q = q_ref[0]
    row = lax.broadcasted_iota(jnp.int32, (n_rows, ATTN_WIDTH), 0)
    lane = lax.broadcasted_iota(jnp.int32, (n_rows, ATTN_WIDTH), 1)
    own_lanes = jnp.right_shift(lane, 6) == jnp.bitwise_and(row, SAMPLE_ROWS - 1)
    q_rows = jnp.concatenate([jnp.broadcast_to(q[i:i + 1, :], (SAMPLE_ROWS, ATTN_WIDTH)) for i in range(t_new)],
                             axis=0)
    qbd = jnp.where(own_lanes, q_rows, 0.0).astype(BF16)

    nt = (((1,), (1,)), ((), ()))
    s_tail = lax.dot_general(qbd, kc_ref[...], nt, preferred_element_type=F32)
    tok = jnp.right_shift(lax.broadcasted_iota(jnp.int32, s_tail.shape, 0), 4)
    col = lax.broadcasted_iota(jnp.int32, s_tail.shape, 1)
    cache_row = col + (past - tail)
    is_cache = col < tail
    (w1, d1), (w4, d4), (w16, d16) = DILATIONS
    in_d1 = is_cache & (cache_row >= past + tok - w1 // d1)
    in_d4 = is_cache & (jnp.bitwise_and(cache_row - tok, d4 - 1) == 0) & (cache_row >= past + tok - w4)
    new_col = col - tail
    is_new = (col >= tail) & (new_col < t_new)
    new_d1 = is_new & (new_col <= tok)
    new_same = is_new & (new_col == tok)
    mult_tail = (in_d1.astype(F32) + in_d4.astype(F32) + new_d1.astype(F32)
                 + 2.0 * new_same.astype(F32))

    s_str = []
    for r in range(t_new):
        k_r = ks_ref[0, :, r * ATTN_WIDTH:(r + 1) * ATTN_WIDTH].astype(BF16)
        s_str.append(lax.dot_general(qbd, k_r, nt, preferred_element_type=F32))
    tok_g = jnp.right_shift(lax.broadcasted_iota(jnp.int32, (n_rows, groups), 0), 4)
    grp = lax.broadcasted_iota(jnp.int32, (n_rows, groups), 1)
    in_d16 = grp >= (past // d16 - w16 // d16)
    mult_str = [(in_d16 & (tok_g == r)).astype(F32) for r in range(t_new)]

    m = jnp.max(jnp.where(mult_tail > 0, s_tail, NEG_INF), axis=1, keepdims=True)
    for r in range(t_new):
        m = jnp.maximum(m, jnp.max(jnp.where(mult_str[r] > 0, s_str[r], NEG_INF), axis=1, keepdims=True))
    p_tail = mult_tail * jnp.exp(jnp.where(mult_tail > 0, s_tail, NEG_INF) - m)
    l = jnp.sum(p_tail, axis=1, keepdims=True)
    o = jnp.dot(p_tail.astype(BF16), vc_ref[...], preferred_element_type=F32)
    for r in range(t_new):
        p_r = mult_str[r] * jnp.exp(jnp.where(mult_str[r] > 0, s_str[r], NEG_INF) - m)
        l = l + jnp.sum(p_r, axis=1, keepdims=True)
        v_r = vs_ref[0, :, r * ATTN_WIDTH:(r + 1) * ATTN_WIDTH].astype(BF16)
        o = o + jnp.dot(p_r.astype(BF16), v_r, preferred_element_type=F32)
    o = jnp.where(own_lanes, o / l, 0.0)
    o_ref[0] = jnp.sum(o.reshape(t_new, SAMPLE_ROWS, ATTN_WIDTH), axis=1).astype(o_ref.dtype)


def _attention_sample(q, k_new, v_new, cache_k, cache_v):
    n_seq, t_new, _ = q.shape
    past = cache_k.shape[1]
    (w1, d1), (w4, d4), (w16, d16) = DILATIONS
    tail = w4
    assert past % d16 == 0 and past >= w16 and past % tail == 0 and t_new <= d4 and w1 <= tail
    groups = past // d16
    ck_g = cache_k.reshape(n_seq, groups, d16 * ATTN_WIDTH)
    cv_g = cache_v.reshape(n_seq, groups, d16 * ATTN_WIDTH)
    new_blk = pl.BlockSpec((1, t_new, ATTN_WIDTH), lambda n: (n, 0, 0))
    tail_blk = pl.BlockSpec((1, tail, ATTN_WIDTH), lambda n: (n, past // tail - 1, 0))
    grp_blk = pl.BlockSpec((1, groups, t_new * ATTN_WIDTH), lambda n: (n, 0, 0))
    return pl.pallas_call(
        functools.partial(_attn_sample_body, t_new=t_new, tail=tail, groups=groups, past=past),
        grid=(n_seq,),
        in_specs=[new_blk, new_blk, new_blk, tail_blk, tail_blk, grp_blk, grp_blk],
        out_specs=new_blk,
        out_shape=jax.ShapeDtypeStruct((n_seq, t_new, ATTN_WIDTH), BF16),
        scratch_shapes=[pltpu.VMEM((tail + 16, ATTN_WIDTH), BF16), pltpu.VMEM((tail + 16, ATTN_WIDTH), BF16)],
        compiler_params=_cparams(("parallel",)),
        name="attention_sample",
    )(q, k_new, v_new, cache_k, cache_v, ck_g, cv_g)


def _ln_swish(y, lg, lb):
    mu = jnp.mean(y, axis=-1, keepdims=True)
    var = jnp.mean(jnp.square(y - mu), axis=-1, keepdims=True)
    yn = (y - mu) * lax.rsqrt(var + LN_EPS) * lg + lb
    return yn * jax.nn.sigmoid(yn)


def _conv_prompt_body(cur_ref, prev_ref, cw_ref, cb_ref, lg_ref, lb_ref, o_ref, win_ref, *, ch):
    j = pl.program_id(1)
    win_ref[0:ch, :] = jnp.where(j == 0, 0.0, prev_ref[...])
    win_ref[ch:2 * ch, :] = cur_ref[...]
    acc = jnp.zeros((ch, CONV_CH), F32)
    for w in range(CONV_WIDTH):
        acc = acc + win_ref[ch - CONV_HIST + w:2 * ch - CONV_HIST + w, :] * cw_ref[w:w + 1, :]
    o_ref[...] = _ln_swish(acc + cb_ref[...], lg_ref[...], lb_ref[...]).astype(o_ref.dtype)


def _conv_prompt(g, cw, cb, lg, lb, batch, seq, ch):
    nj = seq // ch
    par = pl.BlockSpec((1, CONV_CH), lambda b, j: (0, 0))
    return pl.pallas_call(
        functools.partial(_conv_prompt_body, ch=ch),
        grid=(batch, nj),
        in_specs=[pl.BlockSpec((ch, CONV_CH), lambda b, j: (b * nj + j, 0)),
                  pl.BlockSpec((ch, CONV_CH), lambda b, j: (b * nj + jnp.maximum(j - 1, 0), 0)),
                  pl.BlockSpec((CONV_WIDTH, CONV_CH), lambda b, j: (0, 0)), par, par, par],
        out_specs=pl.BlockSpec((ch, CONV_CH), lambda b, j: (b * nj + j, 0)),
        out_shape=jax.ShapeDtypeStruct((batch * seq, CONV_CH), BF16),
        scratch_shapes=[pltpu.VMEM((2 * ch, CONV_CH), F32)],
        compiler_params=_cparams(("parallel", "parallel")),
        name="conv_prompt",
    )(g, g, cw, cb, lg, lb)


def _conv_sample_body(gh_ref, cw_ref, cb_ref, lg_ref, lb_ref, o_ref, *, t_new):
    for i in range(t_new):
        acc = jnp.zeros(gh_ref.shape[1:], F32)
        for w in range(CONV_WIDTH):
            acc = acc + gh_ref[i + w] * cw_ref[w:w + 1, :]
        o_ref[i] = _ln_swish(acc + cb_ref[...], lg_ref[...], lb_ref[...]).astype(o_ref.dtype)


def _conv_sample(gh_t, cw, cb, lg, lb):
    lh, n_seq, _ = gh_t.shape
    t_new = lh - CONV_HIST
    return pl.pallas_call(
        functools.partial(_conv_sample_body, t_new=t_new),
        out_shape=jax.ShapeDtypeStruct((t_new, n_seq, CONV_CH), BF16),
        name="conv_sample",
    )(gh_t, cw, cb, lg, lb)


def _outproj_body(attn_ref, c_ref, x_ref, wo_ref, g2_ref, wq_ref, keys_ref, x1_ref, h2t_ref, st_ref):
    a = jnp.dot(attn_ref[...], wo_ref[0:ATTN_WIDTH, :], preferred_element_type=F32)
    a = a + jnp.dot(c_ref[...], wo_ref[ATTN_WIDTH:D_MODEL, :], preferred_element_type=F32)
    x1 = x_ref[...] + a
    x1_ref[...] = x1
    h2 = _rms(x1, g2_ref[...])
    h2t_ref[...] = h2.T.astype(BF16)
    qv = jnp.dot(h2.astype(BF16), wq_ref[...], preferred_element_type=F32).astype(BF16)
    nt = (((1,), (1,)), ((), ()))
    for h in range(PEER_HEADS):
        for side in range(2):
            qs = qv[:, h * D_QUERY + side * HALF_Q:h * D_QUERY + (side + 1) * HALF_Q]
            st_ref[2 * h + side] = lax.dot_general(keys_ref[side], qs, nt, preferred_element_type=F32)


def _out_projection(attn, c, x, wo_bf16, g2, wq_bf16, keys_bf16, tm):
    T = x.shape[0]
    tok = lambda i: (i, 0)
    full2 = lambda i: (0, 0)
    return pl.pallas_call(
        _outproj_body,
        grid=(T // tm,),
        in_specs=[pl.BlockSpec((tm, ATTN_WIDTH), tok), pl.BlockSpec((tm, CONV_CH), tok),
                  pl.BlockSpec((tm, D_MODEL), tok), pl.BlockSpec((D_MODEL, D_MODEL), full2),
                  pl.BlockSpec((1, D_MODEL), full2), pl.BlockSpec((D_MODEL, PEER_HEADS * D_QUERY), full2),
                  pl.BlockSpec((2, N_KEYS, HALF_Q), lambda i: (0, 0, 0))],
        out_specs=[pl.BlockSpec((tm, D_MODEL), tok), pl.BlockSpec((D_MODEL, tm), lambda i: (0, i)),
                   pl.BlockSpec((2 * PEER_HEADS, N_KEYS, tm), lambda i: (0, 0, i))],
        out_shape=[jax.ShapeDtypeStruct((T, D_MODEL), F32), jax.ShapeDtypeStruct((D_MODEL, T), BF16),
                   jax.ShapeDtypeStruct((2 * PEER_HEADS, N_KEYS, T), F32)],
        compiler_params=_cparams(("parallel",)),
        name="out_projection",
    )(attn, c, x, wo_bf16, g2, wq_bf16, keys_bf16)


def _top16(s, order):
    rank = jnp.full(s.shape, NOT_SELECTED, F32)
    slot = lax.broadcasted_iota(jnp.int32, (PEER_TOPK, s.shape[1]), 0)
    vals = jnp.zeros((PEER_TOPK, s.shape[1]), F32)
    for a in range(PEER_TOPK):
        mx = jnp.max(s, axis=0, keepdims=True)
        first = jnp.min(jnp.where(s == mx, order, 1e9), axis=0, keepdims=True)
        sel = order == first
        rank = jnp.where(sel, float(a), rank)
        s = jnp.where(sel, -jnp.inf, s)
        vals = jnp.where(slot == a, mx, vals)
    return rank, vals


def _peer_select_body(st_ref, p1_ref, c1_ref, p2_ref, r2_ref, *, tl):
    K = PEER_TOPK
    key_order = lax.broadcasted_iota(jnp.int32, (N_KEYS, LANES), 0).astype(F32)
    half = K // 2
    sub = lax.broadcasted_iota(jnp.int32, (half, LANES), 0)
    flat = [lax.broadcasted_iota(jnp.int32, (K, LANES), 0)]
    flat += [a * K + sub for a in range(1, half)]
    flat += [(half + sub) * K]
    cand_order = jnp.concatenate(flat, axis=0).astype(F32)
    slot16 = lax.broadcasted_iota(jnp.int32, (K, LANES), 0)

    def one(idx, carry):
        h = idx // (tl // LANES)
        ls = pl.multiple_of((idx % (tl // LANES)) * LANES, LANES)
        lanes = pl.ds(ls, LANES)
        s1 = st_ref[2 * h, :, lanes]
        s2 = st_ref[2 * h + 1, :, lanes]
        rank1, v1 = _top16(s1, key_order)
        rank2, v2 = _top16(s2, key_order)
        cand = [v1[0:1, :] + v2]
        cand += [v1[a:a + 1, :] + v2[0:half, :] for a in range(1, half)]
        cand += [v1[half:K, :] + v2[0:1, :]]
        cand = jnp.concatenate(cand, axis=0)
        crank, _ = _top16(cand, cand_order)
        chosen = crank < float(K)
        top = v1[0:1, :] + v2[0:1, :]
        z = jnp.sum(jnp.where(chosen, jnp.exp(cand - top), 0.0), axis=0, keepdims=True)
        chosen_f = chosen.astype(F32)
        count = jnp.zeros((K, LANES), F32)
        count = jnp.where(slot16 == 0, jnp.sum(chosen_f[0:K], axis=0, keepdims=True), count)
        for a in range(1, half):
            lo = K + (a - 1) * half
            count = jnp.where(slot16 == a, jnp.sum(chosen_f[lo:lo + half], axis=0, keepdims=True), count)
        count = jnp.concatenate([count[0:half], chosen_f[K + (half - 1) * half:]], axis=0)
        width = jnp.zeros((N_KEYS, LANES), F32)
        for a in range(K):
            width = jnp.where(rank1 == float(a), count[a:a + 1, :], width)
        in1 = rank1 < float(K)
        in2 = rank2 < float(K)
        p1_ref[h, :, lanes] = jnp.where(in1, jnp.exp(s1 - v1[0:1, :]) / z, 0.0)
        c1_ref[h, :, lanes] = width
        p2_ref[h, :, lanes] = jnp.where(in2, jnp.exp(s2 - v2[0:1, :]), 0.0)
        r2_ref[h, :, lanes] = rank2
        return carry

    lax.fori_loop(0, PEER_HEADS * (tl // LANES), one, 0)


def _peer_select(st, tl):
    T = st.shape[2]
    out_blk = pl.BlockSpec((PEER_HEADS, N_KEYS, tl), lambda i: (0, 0, i))
    shp = jax.ShapeDtypeStruct((PEER_HEADS, N_KEYS, T), F32)
    return pl.pallas_call(
        functools.partial(_peer_select_body, tl=tl),
        grid=(T // tl,),
        in_specs=[pl.BlockSpec((2 * PEER_HEADS, N_KEYS, tl), lambda i: (0, 0, i))],
        out_specs=[out_blk] * 4,
        out_shape=[shp] * 4,
        compiler_params=_cparams(("parallel",)),
        name="peer_select",
    )(st)


PEER_SUB = 256


def _peer_body(h2t_ref, p1_ref, c1_ref, p2_ref, r2_ref, u_ref, vt_ref, o_ref, *, te):
    e = pl.program_id(1)

    @pl.when(e == 0)
    def _():
        o_ref[...] = jnp.zeros(o_ref.shape, F32)

    h2t = h2t_ref[...]
    acc = o_ref[...]
    for c in range(te // PEER_SUB):
        a = jnp.dot(u_ref[c * PEER_SUB:(c + 1) * PEER_SUB, :], h2t, preferred_element_type=F32)
        gates = []
        for jj in range(PEER_SUB // N_KEYS):
            j = c * (PEER_SUB // N_KEYS) + jj
            w = jnp.zeros((N_KEYS, a.shape[1]), F32)
            for h in range(PEER_HEADS):
                keep = r2_ref[h] < c1_ref[h, j:j + 1, :]
                w = w + jnp.where(keep, p2_ref[h], 0.0) * p1_ref[h, j:j + 1, :]
            gates.append(w)
        wg = (jnp.concatenate(gates, axis=0) * jax.nn.gelu(a, approximate=True)).astype(BF16)
        acc = acc + jnp.dot(vt_ref[:, c * PEER_SUB:(c + 1) * PEER_SUB], wg, preferred_element_type=F32)
    o_ref[...] = acc


def _peer_experts(h2t, p1, c1, p2, r2, u_bf16, vt_bf16, tm, te):
    T = h2t.shape[1]
    firsts = te // N_KEYS
    sel_all = pl.BlockSpec((PEER_HEADS, N_KEYS, tm), lambda i, e: (0, 0, i))
    sel_blk = pl.BlockSpec((PEER_HEADS, firsts, tm), lambda i, e: (0, e, i))
    return pl.pallas_call(
        functools.partial(_peer_body, te=te),
        grid=(T // tm, N_EXPERTS // te),
        in_specs=[pl.BlockSpec((D_MODEL, tm), lambda i, e: (0, i)), sel_blk, sel_blk, sel_all, sel_all,
                  pl.BlockSpec((te, D_MODEL), lambda i, e: (e, 0)),
                  pl.BlockSpec((D_MODEL, te), lambda i, e: (0, e))],
        out_specs=pl.BlockSpec((D_MODEL, tm), lambda i, e: (0, i)),
        out_shape=jax.ShapeDtypeStruct((D_MODEL, T), F32),
        compiler_params=_cparams(("parallel", "arbitrary")),
        name="peer_experts",
    )(h2t, p1, c1, p2, r2, u_bf16, vt_bf16)


def _final_body(x1_ref, ot_ref, g_ref, y_ref):
    y_ref[...] = _rms(x1_ref[...] + ot_ref[...].T, g_ref[...])


def _final_norm(x1, out_t, g, tm):
    T = x1.shape[0]
    return pl.pallas_call(
        _final_body,
        grid=(T // tm,),
        in_specs=[pl.BlockSpec((tm, D_MODEL), lambda i: (i, 0)), pl.BlockSpec((D_MODEL, tm), lambda i: (0, i)),
                  pl.BlockSpec((1, D_MODEL), lambda i: (0, 0))],
        out_specs=pl.BlockSpec((tm, D_MODEL), lambda i: (i, 0)),
        out_shape=jax.ShapeDtypeStruct((T, D_MODEL), F32),
        compiler_params=_cparams(("parallel",)),
        name="final_norm",
    )(x1, out_t, g)


TOKEN_TILE = 512
PEER_EXPERT_TILE = 1024
CONV_ROWS = 512


def _channel_and_final(attn, c, x, p):
    x1, h2t, st = _out_projection(attn, c, x, p["wo"], p["g2"], p["wq"], p["keys"], TOKEN_TILE)
    p1, c1, p2, r2 = _peer_select(st, TOKEN_TILE)
    out_t = _peer_experts(h2t, p1, c1, p2, r2, p["u"], p["vt"], TOKEN_TILE, PEER_EXPERT_TILE)
    return _final_norm(x1, out_t, p["gf"], TOKEN_TILE)


def kernel(x_prompt, x_sample, cache_k, cache_v, state_conv, norm1_g, w_in, conv_w, conv_b, conv_ln_g,
           conv_ln_b, w_out, norm2_g, w_query, sub_keys1, sub_keys2, expert_u, expert_v, final_g):
    bp, s, _ = x_prompt.shape
    bs, t, _ = x_sample.shape
    depth, _, w_buf = cache_k.shape[:3]
    assert depth == 1 and all(s % (d * BAND_BLOCK) == 0 and s // d >= 2 * BAND_BLOCK for _, d in DILATIONS)
    keep = min(DILATIONS[-1][0], s)
    l = 0
    row = lambda a: a.reshape(1, -1)
    p = dict(
        wo=w_out[l].astype(BF16), g2=row(norm2_g[l]), wq=w_query[l].astype(BF16),
        keys=jnp.stack([sub_keys1[l], sub_keys2[l]]).astype(BF16),
        u=expert_u[l].astype(BF16), vt=expert_v[l].T.astype(BF16), gf=row(final_g),
    )
    w_in_b = w_in[l].astype(BF16)
    g1 = row(norm1_g[l])
    cw, cb, lg, lb = conv_w[l], row(conv_b[l]), row(conv_ln_g[l]), row(conv_ln_b[l])

    xp = x_prompt.reshape(bp * s, D_MODEL)
    cos, sin = _rope_tables(jnp.tile(jnp.arange(s), bp))
    q, k, v, g = _in_projection(xp, g1, w_in_b, cos, sin, TOKEN_TILE)
    attn = _attention_prompt(q, k, v, bp, s)
    c = _conv_prompt(g, cw, cb, lg, lb, bp, s, CONV_ROWS)
    y_prompt = _channel_and_final(attn, c, xp, p).reshape(bp, s, D_MODEL)
    new_k_prompt = k.reshape(bp, s, N_HEADS, HEAD_DIM)[None, :, s - keep:]
    new_v_prompt = v.reshape(bp, s, N_HEADS, HEAD_DIM)[None, :, s - keep:]
    new_conv_prompt = g.reshape(bp, s, CONV_CH)[None, :, s - CONV_HIST:]

    xs = x_sample.reshape(bs * t, D_MODEL)
    assert w_buf == min(MAX_WINDOW, PAST_LEN)
    cos, sin = _rope_tables(jnp.tile(PAST_LEN + jnp.arange(t), bs))
    q, k, v, g = _in_projection(xs, g1, w_in_b, cos, sin, TOKEN_TILE)
    seq3 = lambda a: a.reshape(bs, t, ATTN_WIDTH)
    attn = _attention_sample(seq3(q), seq3(k), seq3(v), cache_k[l].reshape(bs, w_buf, ATTN_WIDTH),
                             cache_v[l].reshape(bs, w_buf, ATTN_WIDTH)).reshape(bs * t, ATTN_WIDTH)
    g_hist = jnp.concatenate([state_conv[l], g.reshape(bs, t, CONV_CH)], axis=1)
    c = _conv_sample(jnp.transpose(g_hist, (1, 0, 2)), cw, cb, lg, lb)
    c = jnp.transpose(c, (1, 0, 2)).reshape(bs * t, CONV_CH)
    y_sample = _channel_and_final(attn, c, xs, p).reshape(bs, t, D_MODEL)
    new_k_sample = k.reshape(bs, t, N_HEADS, HEAD_DIM)[None]
    new_v_sample = v.reshape(bs, t, N_HEADS, HEAD_DIM)[None]
    new_conv_sample = g_hist[None, :, -CONV_HIST:]

    return (y_prompt, y_sample, new_k_prompt, new_v_prompt, new_conv_prompt,
            new_k_sample, new_v_sample, new_conv_sample)
```

```python
import functools

import jax
import jax.numpy as jnp
from jax import lax
from jax.experimental import pallas as pl
from jax.experimental.pallas import tpu as pltpu

F32 = jnp.float32
BF16 = jnp.bfloat16

D_MODEL = 1024
HEAD_DIM = 64
HALF_HEAD = HEAD_DIM // 2
N_HEADS = 12
ATTN_WIDTH = N_HEADS * HEAD_DIM
CONV_CH = D_MODEL - ATTN_WIDTH
CONV_WIDTH = 31
CONV_HIST = CONV_WIDTH - 1
IN_COLS = 3 * ATTN_WIDTH + 2 * CONV_CH
DILATIONS = ((128, 1), (512, 4), (2048, 16))
MAX_WINDOW = 2048
PAST_LEN = 2048
BAND_BLOCK = 128
ROPE_THETA = 10000.0
ATTN_SCALE = HEAD_DIM ** -0.5
NEG_INF = -1e30
N_KEYS = 128
N_EXPERTS = N_KEYS * N_KEYS
PEER_HEADS = 8
PEER_TOPK = 16
D_QUERY = 256
HALF_Q = D_QUERY // 2
RMS_EPS = 1e-6
LN_EPS = 1e-5

LANES = 128
HEADS_PER_LANE_TILE = LANES // HEAD_DIM
N_HEAD_TILES = ATTN_WIDTH // LANES
NOT_SELECTED = 99.0
VMEM_LIMIT = 56 * 1024 * 1024


def _cparams(semantics, flags=None):
    return pltpu.CompilerParams(dimension_semantics=semantics, vmem_limit_bytes=VMEM_LIMIT, flags=flags)


def _rms(x, g):
    return (x * lax.rsqrt(jnp.mean(x * x, axis=-1, keepdims=True) + RMS_EPS)) * g


def _inproj_body(x_ref, g_ref, w_ref, cos_ref, sin_ref, q_ref, k_ref, v_ref, gl_ref):
    hb = _rms(x_ref[...], g_ref[...]).astype(BF16)
    cos = cos_ref[...]
    sin = sin_ref[...]
    lane = lax.broadcasted_iota(jnp.int32, cos.shape, 1)
    first_half = jnp.bitwise_and(lane, HEAD_DIM - 1) < HALF_HEAD

    def rope(t):
        partner = jnp.where(first_half, pltpu.roll(t, LANES - HALF_HEAD, 1), pltpu.roll(t, HALF_HEAD, 1))
        return t * cos + partner * sin

    q = jnp.dot(hb, w_ref[:, 0:ATTN_WIDTH], preferred_element_type=F32)
    k = jnp.dot(hb, w_ref[:, ATTN_WIDTH:2 * ATTN_WIDTH], preferred_element_type=F32)
    for c in range(N_HEAD_TILES):
        sl = slice(c * LANES, (c + 1) * LANES)
        q_ref[:, sl] = rope(q[:, sl]) * ATTN_SCALE
        k_ref[:, sl] = rope(k[:, sl])
    v_ref[...] = jnp.dot(hb, w_ref[:, 2 * ATTN_WIDTH:3 * ATTN_WIDTH], preferred_element_type=F32)
    u = jnp.dot(hb, w_ref[:, 3 * ATTN_WIDTH:IN_COLS], preferred_element_type=F32)
    gl_ref[...] = u[:, :CONV_CH] * jax.nn.sigmoid(u[:, CONV_CH:])


def _in_projection(x, g, w_bf16, cos, sin, tm):
    T = x.shape[0]
    tok = lambda i: (i, 0)
    full = lambda i: (0, 0)
    return pl.pallas_call(
        _inproj_body,
        grid=(T // tm,),
        in_specs=[pl.BlockSpec((tm, D_MODEL), tok), pl.BlockSpec((1, D_MODEL), full),
                  pl.BlockSpec((D_MODEL, IN_COLS), full),
                  pl.BlockSpec((tm, LANES), tok), pl.BlockSpec((tm, LANES), tok)],
        out_specs=[pl.BlockSpec((tm, ATTN_WIDTH), tok)] * 3 + [pl.BlockSpec((tm, CONV_CH), tok)],
        out_shape=[jax.ShapeDtypeStruct((T, ATTN_WIDTH), F32)] * 3 + [jax.ShapeDtypeStruct((T, CONV_CH), F32)],
        compiler_params=_cparams(("parallel",)),
        name="in_projection",
    )(x, g, w_bf16, cos, sin)


def _rope_tables(pos):
    inv = jnp.power(ROPE_THETA, -jnp.arange(HALF_HEAD, dtype=F32) * (2.0 / HEAD_DIM))
    ang = pos.astype(F32)[:, None] * inv[None, :]
    cos, sin = jnp.cos(ang), jnp.sin(ang)
    return (jnp.concatenate([cos, cos] * HEADS_PER_LANE_TILE, axis=-1),
            jnp.concatenate([-sin, sin] * HEADS_PER_LANE_TILE, axis=-1))


ATTN_MERGE_ROWS = 256


def _attn_prompt_body(q_ref, k_ref, v_ref, o_ref, ob_ref, mb_ref, lb_ref, *, seq):
    Q = BAND_BLOCK
    lane = lax.broadcasted_iota(jnp.int32, (Q, LANES), 1)
    head_lanes = [lane < HEAD_DIM, lane >= HEAD_DIM]
    qi = lax.broadcasted_iota(jnp.int32, (Q, 2 * Q), 0)
    ki = lax.broadcasted_iota(jnp.int32, (Q, 2 * Q), 1)

    for b, (win, dil) in enumerate(DILATIONS):
        reach = win // dil
        n_blocks = (seq // dil) // Q

        def rows(start, size, dil=dil):
            return pl.ds(start, size) if dil == 1 else pl.ds(start, size, stride=dil)

        def step(idx, carry, b=b, dil=dil, reach=reach, n_blocks=n_blocks, rows=rows):
            r = idx // n_blocks
            a0 = (idx % n_blocks) * Q
            ks = jnp.maximum(a0 - Q, 0)
            q_rows = rows(r + a0 * dil, Q)
            k_rows = rows(r + ks * dil, 2 * Q)
            q = q_ref[q_rows, :]
            k = k_ref[k_rows, :].astype(BF16)
            v = v_ref[k_rows, :].astype(BF16)
            dist = (a0 - ks) + qi - ki
            valid = (dist >= 0) & (dist <= reach)
            o_full = jnp.zeros((Q, LANES), F32)
            m_full = jnp.zeros((Q, LANES), F32)
            l_full = jnp.zeros((Q, LANES), F32)
            for h in range(HEADS_PER_LANE_TILE):
                qh = jnp.where(head_lanes[h], q, 0.0).astype(BF16)
                s = lax.dot_general(qh, k, (((1,), (1,)), ((), ())), preferred_element_type=F32)
                s = jnp.where(valid, s, NEG_INF)
                m = jnp.max(s, axis=1, keepdims=True)
                p = jnp.exp(s - m)
                l = jnp.sum(p, axis=1, keepdims=True)
                pv = jnp.dot(p.astype(BF16), v, preferred_element_type=F32)
                o_full = jnp.where(head_lanes[h], pv, o_full)
                m_full = jnp.where(head_lanes[h], m, m_full)
                l_full = jnp.where(head_lanes[h], l, l_full)
            ob_ref[b, q_rows, :] = o_full
            mb_ref[b, q_rows, :] = m_full
            lb_ref[b, q_rows, :] = l_full
            return carry

        lax.fori_loop(0, dil * n_blocks, step, 0, unroll=8)

    def merge(i, carry):
        rws = pl.ds(pl.multiple_of(i * ATTN_MERGE_ROWS, ATTN_MERGE_ROWS), ATTN_MERGE_ROWS)
        n_br = len(DILATIONS)
        m_all = mb_ref[0, rws, :]
        for b in range(1, n_br):
            m_all = jnp.maximum(m_all, mb_ref[b, rws, :])
        num = jnp.zeros((ATTN_MERGE_ROWS, LANES), F32)
        den = jnp.zeros((ATTN_MERGE_ROWS, LANES), F32)
        for b in range(n_br):
            w = jnp.exp(mb_ref[b, rws, :] - m_all)
            num = num + w * ob_ref[b, rws, :]
            den = den + w * lb_ref[b, rws, :]
        o_ref[rws, :] = (num / den).astype(o_ref.dtype)
        return carry

    lax.fori_loop(0, seq // ATTN_MERGE_ROWS, merge, 0)


def _attention_prompt(q, k, v, batch, seq):
    T = q.shape[0]
    blk = pl.BlockSpec((seq, LANES), lambda b, hp: (b, hp))
    return pl.pallas_call(
        functools.partial(_attn_prompt_body, seq=seq),
        grid=(batch, N_HEAD_TILES),
        in_specs=[blk, blk, blk],
        out_specs=blk,
        out_shape=jax.ShapeDtypeStruct((T, ATTN_WIDTH), BF16),
        scratch_shapes=[pltpu.VMEM((len(DILATIONS), seq, LANES), F32)] * 3,
        compiler_params=_cparams(("parallel", "parallel")),
        name="attention_prompt",
    )(q, k, v)


SAMPLE_CHUNK = 128


def _attn_sample_body(q_ref, kn_ref, vn_ref, kt_ref, vt_ref, kg_ref, vg_ref, o_ref, s_ref, p_ref, m_ref, l_ref,
                      acc_ref, *, t_new, tail, past):
    c = pl.program_id(1)
    n_seq = q_ref.shape[-1]
    (w1, d1), (w4, d4), (w16, d16) = DILATIONS
    q = [q_ref[i] for i in range(t_new)]

    @pl.when(c == 0)
    def _():
        for i in range(t_new):
            s = [jnp.sum(q[i] * kn_ref[j], axis=0, keepdims=True) for j in range(i + 1)]
            m = s[0]
            for j in range(1, i + 1):
                m = jnp.maximum(m, s[j])
            l = jnp.zeros((1, n_seq), F32)
            acc = jnp.zeros((HEAD_DIM, n_seq), F32)
            for j in range(i + 1):
                p = (float(len(DILATIONS)) if j == i else 1.0) * jnp.exp(s[j] - m)
                l = l + p
                acc = acc + p * vn_ref[j]
            m_ref[i] = m
            l_ref[i] = l
            acc_ref[i] = acc

    def chunk(k_row_of, v_row_of):
        def score_row(r, carry):
            k_row = k_row_of(r)
            for i in range(t_new):
                s_ref[i, pl.ds(r, 1), :] = jnp.sum(q[i] * k_row, axis=0, keepdims=True)
            return carry

        lax.fori_loop(0, SAMPLE_CHUNK, score_row, 0, unroll=2)

        slot = c * SAMPLE_CHUNK + lax.broadcasted_iota(jnp.int32, (SAMPLE_CHUNK, n_seq), 0)
        in_tail = slot < tail
        cache_row = slot + (past - tail)
        grp = jnp.right_shift(slot - tail, shift)
        grp_row = jnp.bitwise_and(slot - tail, low)
        for i in range(t_new):
            in_d1 = in_tail & (cache_row >= past + i - w1 // d1)
            in_d4 = in_tail & (jnp.bitwise_and(cache_row - i, d4 - 1) == 0) & (cache_row >= past + i - w4)
            in_d16 = (~in_tail) & (grp_row == i) & (grp >= past // d16 - w16 // d16)
            mult = in_d1.astype(F32) + in_d4.astype(F32) + in_d16.astype(F32)
            s = jnp.where(mult > 0, s_ref[i], NEG_INF)
            m_old = m_ref[i]
            m_new = jnp.maximum(m_old, jnp.max(s, axis=0, keepdims=True))
            alpha = jnp.exp(m_old - m_new)
            p = mult * jnp.exp(s - m_new)
            p_ref[i] = p
            l_ref[i] = alpha * l_ref[i] + jnp.sum(p, axis=0, keepdims=True)
            m_ref[i] = m_new
            acc_ref[i] = acc_ref[i] * alpha

        def value_row(r, acc):
            v_row = v_row_of(r)
            return tuple(acc[i] + p_ref[i, pl.ds(r, 1), :] * v_row for i in range(t_new))

        acc = lax.fori_loop(0, SAMPLE_CHUNK, value_row, tuple(acc_ref[i] for i in range(t_new)), unroll=2)
        for i in range(t_new):
            acc_ref[i] = acc[i]

    shift, low = t_new.bit_length() - 1, t_new - 1

    @pl.when(c < tail // SAMPLE_CHUNK)
    def _():
        chunk(lambda r: kt_ref[r], lambda r: vt_ref[r])

    @pl.when(c >= tail // SAMPLE_CHUNK)
    def _():
        chunk(lambda r: kg_ref[jnp.right_shift(r, shift), jnp.bitwise_and(r, low)],
              lambda r: vg_ref[jnp.right_shift(r, shift), jnp.bitwise_and(r, low)])

    @pl.when(c == pl.num_programs(1) - 1)
    def _():
        for i in range(t_new):
            o_ref[i] = acc_ref[i] / l_ref[i]


def _attention_sample(q, k_new, v_new, cache_k, cache_v):
    n_seq, t_new = q.shape[:2]
    past = cache_k.shape[1]
    (w1, d1), (w4, d4), (w16, d16) = DILATIONS
    tail = w4
    assert past % d16 == 0 and past >= w16 and w1 <= tail and t_new <= d4 and t_new & (t_new - 1) == 0
    tail_chunks = tail // SAMPLE_CHUNK
    group_chunks = (past // d16 * t_new) // SAMPLE_CHUNK
    groups_per_chunk = SAMPLE_CHUNK // t_new
    assert tail % SAMPLE_CHUNK == 0 and past % SAMPLE_CHUNK == 0 and (past // d16) % groups_per_chunk == 0
    lanes_last = lambda a: jnp.transpose(a, (1, 2, 3, 0))
    grouped = lambda a: a.reshape((past // d16, d16) + a.shape[1:])
    ck, cv = lanes_last(cache_k), lanes_last(cache_v)
    new_blk = pl.BlockSpec((t_new, None, HEAD_DIM, n_seq), lambda h, c: (0, h, 0, 0))
    tail_blk = pl.BlockSpec((SAMPLE_CHUNK, None, HEAD_DIM, n_seq),
                            lambda h, c: (past // SAMPLE_CHUNK - tail_chunks + jnp.minimum(c, tail_chunks - 1), h, 0, 0))
    group_blk = pl.BlockSpec((groups_per_chunk, t_new, None, HEAD_DIM, n_seq),
                             lambda h, c: (jnp.maximum(c - tail_chunks, 0), 0, h, 0, 0))
    out = pl.pallas_call(
        functools.partial(_attn_sample_body, t_new=t_new, tail=tail, past=past),
        grid=(N_HEADS, tail_chunks + group_chunks),
        in_specs=[new_blk, new_blk, new_blk, tail_blk, tail_blk, group_blk, group_blk],
        out_specs=pl.BlockSpec((None, t_new, HEAD_DIM, n_seq), lambda h, c: (h, 0, 0, 0)),
        out_shape=jax.ShapeDtypeStruct((N_HEADS, t_new, HEAD_DIM, n_seq), F32),
        scratch_shapes=[pltpu.VMEM((t_new, SAMPLE_CHUNK, n_seq), F32), pltpu.VMEM((t_new, SAMPLE_CHUNK, n_seq), F32),
                        pltpu.VMEM((t_new, 1, n_seq), F32), pltpu.VMEM((t_new, 1, n_seq), F32),
                        pltpu.VMEM((t_new, HEAD_DIM, n_seq), F32)],
        compiler_params=_cparams(("parallel", "arbitrary")),
        name="attention_sample",
    )(lanes_last(q), lanes_last(k_new), lanes_last(v_new), ck, cv, grouped(ck), grouped(cv))
    return jnp.transpose(out, (3, 1, 0, 2)).reshape(n_seq, t_new, ATTN_WIDTH)


def _ln_swish(y, lg, lb):
    mu = jnp.mean(y, axis=-1, keepdims=True)
    var = jnp.mean(jnp.square(y - mu), axis=-1, keepdims=True)
    yn = (y - mu) * lax.rsqrt(var + LN_EPS) * lg + lb
    return yn * jax.nn.sigmoid(yn)


def _conv_prompt_body(cur_ref, prev_ref, cw_ref, cb_ref, lg_ref, lb_ref, o_ref, win_ref, *, ch):
    j = pl.program_id(1)
    win_ref[0:ch, :] = jnp.where(j == 0, 0.0, prev_ref[...])
    win_ref[ch:2 * ch, :] = cur_ref[...]
    acc = jnp.zeros((ch, CONV_CH), F32)
    for w in range(CONV_WIDTH):
        acc = acc + win_ref[ch - CONV_HIST + w:2 * ch - CONV_HIST + w, :] * cw_ref[w:w + 1, :]
    o_ref[...] = _ln_swish(acc + cb_ref[...], lg_ref[...], lb_ref[...]).astype(o_ref.dtype)


def _conv_prompt(g, cw, cb, lg, lb, batch, seq, ch):
    nj = seq // ch
    par = pl.BlockSpec((1, CONV_CH), lambda b, j: (0, 0))
    return pl.pallas_call(
        functools.partial(_conv_prompt_body, ch=ch),
        grid=(batch, nj),
        in_specs=[pl.BlockSpec((ch, CONV_CH), lambda b, j: (b * nj + j, 0)),
                  pl.BlockSpec((ch, CONV_CH), lambda b, j: (b * nj + jnp.maximum(j - 1, 0), 0)),
                  pl.BlockSpec((CONV_WIDTH, CONV_CH), lambda b, j: (0, 0)), par, par, par],
        out_specs=pl.BlockSpec((ch, CONV_CH), lambda b, j: (b * nj + j, 0)),
        out_shape=jax.ShapeDtypeStruct((batch * seq, CONV_CH), BF16),
        scratch_shapes=[pltpu.VMEM((2 * ch, CONV_CH), F32)],
        compiler_params=_cparams(("parallel", "parallel")),
        name="conv_prompt",
    )(g, g, cw, cb, lg, lb)


def _conv_sample_body(gh_ref, cw_ref, cb_ref, lg_ref, lb_ref, o_ref, *, t_new):
    for i in range(t_new):
        acc = jnp.zeros(gh_ref.shape[1:], F32)
        for w in range(CONV_WIDTH):
            acc = acc + gh_ref[i + w] * cw_ref[w:w + 1, :]
        o_ref[i] = _ln_swish(acc + cb_ref[...], lg_ref[...], lb_ref[...]).astype(o_ref.dtype)


def _conv_sample(gh_t, cw, cb, lg, lb):
    lh, n_seq, _ = gh_t.shape
    t_new = lh - CONV_HIST
    return pl.pallas_call(
        functools.partial(_conv_sample_body, t_new=t_new),
        out_shape=jax.ShapeDtypeStruct((t_new, n_seq, CONV_CH), BF16),
        name="conv_sample",
    )(gh_t, cw, cb, lg, lb)


def _outproj_body(attn_ref, c_ref, x_ref, wo_ref, g2_ref, wq_ref, keys_ref, x1_ref, h2t_ref, st_ref):
    a = jnp.dot(attn_ref[...], wo_ref[0:ATTN_WIDTH, :], preferred_element_type=F32)
    a = a + jnp.dot(c_ref[...], wo_ref[ATTN_WIDTH:D_MODEL, :], preferred_element_type=F32)
    x1 = x_ref[...] + a
    x1_ref[...] = x1
    h2 = _rms(x1, g2_ref[...])
    h2t_ref[...] = h2.T.astype(BF16)
    qv = jnp.dot(h2.astype(BF16), wq_ref[...], preferred_element_type=F32).astype(BF16)
    nt = (((1,), (1,)), ((), ()))
    for h in range(PEER_HEADS):
        for side in range(2):
            qs = qv[:, h * D_QUERY + side * HALF_Q:h * D_QUERY + (side + 1) * HALF_Q]
            st_ref[2 * h + side] = lax.dot_general(keys_ref[side], qs, nt, preferred_element_type=F32)


def _out_projection(attn, c, x, wo_bf16, g2, wq_bf16, keys_bf16, tm):
    T = x.shape[0]
    tok = lambda i: (i, 0)
    full2 = lambda i: (0, 0)
    return pl.pallas_call(
        _outproj_body,
        grid=(T // tm,),
        in_specs=[pl.BlockSpec((tm, ATTN_WIDTH), tok), pl.BlockSpec((tm, CONV_CH), tok),
                  pl.BlockSpec((tm, D_MODEL), tok), pl.BlockSpec((D_MODEL, D_MODEL), full2),
                  pl.BlockSpec((1, D_MODEL), full2), pl.BlockSpec((D_MODEL, PEER_HEADS * D_QUERY), full2),
                  pl.BlockSpec((2, N_KEYS, HALF_Q), lambda i: (0, 0, 0))],
        out_specs=[pl.BlockSpec((tm, D_MODEL), tok), pl.BlockSpec((D_MODEL, tm), lambda i: (0, i)),
                   pl.BlockSpec((2 * PEER_HEADS, N_KEYS, tm), lambda i: (0, 0, i))],
        out_shape=[jax.ShapeDtypeStruct((T, D_MODEL), F32), jax.ShapeDtypeStruct((D_MODEL, T), BF16),
                   jax.ShapeDtypeStruct((2 * PEER_HEADS, N_KEYS, T), F32)],
        compiler_params=_cparams(("parallel",)),
        name="out_projection",
    )(attn, c, x, wo_bf16, g2, wq_bf16, keys_bf16)


def _top16(s, order):
    rank = jnp.full(s.shape, NOT_SELECTED, F32)
    slot = lax.broadcasted_iota(jnp.int32, (PEER_TOPK, s.shape[1]), 0)
    vals = jnp.zeros((PEER_TOPK, s.shape[1]), F32)
    for a in range(PEER_TOPK):
        mx = jnp.max(s, axis=0, keepdims=True)
        first = jnp.min(jnp.where(s == mx, order, 1e9), axis=0, keepdims=True)
        sel = order == first
        rank = jnp.where(sel, float(a), rank)
        s = jnp.where(sel, -jnp.inf, s)
        vals = jnp.where(slot == a, mx, vals)
    return rank, vals


def _peer_select_body(st_ref, p1_ref, c1_ref, p2_ref, r2_ref, *, tl):
    K = PEER_TOPK
    key_order = lax.broadcasted_iota(jnp.int32, (N_KEYS, LANES), 0).astype(F32)
    half = K // 2
    sub = lax.broadcasted_iota(jnp.int32, (half, LANES), 0)
    flat = [lax.broadcasted_iota(jnp.int32, (K, LANES), 0)]
    flat += [a * K + sub for a in range(1, half)]
    flat += [(half + sub) * K]
    cand_order = jnp.concatenate(flat, axis=0).astype(F32)
    slot16 = lax.broadcasted_iota(jnp.int32, (K, LANES), 0)

    def one(idx, carry):
        h = idx // (tl // LANES)
        ls = pl.multiple_of((idx % (tl // LANES)) * LANES, LANES)
        lanes = pl.ds(ls, LANES)
        s1 = st_ref[2 * h, :, lanes]
        s2 = st_ref[2 * h + 1, :, lanes]
        rank1, v1 = _top16(s1, key_order)
        rank2, v2 = _top16(s2, key_order)
        cand = [v1[0:1, :] + v2]
        cand += [v1[a:a + 1, :] + v2[0:half, :] for a in range(1, half)]
        cand += [v1[half:K, :] + v2[0:1, :]]
        cand = jnp.concatenate(cand, axis=0)
        crank, _ = _top16(cand, cand_order)
        chosen = crank < float(K)
        top = v1[0:1, :] + v2[0:1, :]
        z = jnp.sum(jnp.where(chosen, jnp.exp(cand - top), 0.0), axis=0, keepdims=True)
        chosen_f = chosen.astype(F32)
        count = jnp.zeros((K, LANES), F32)
        count = jnp.where(slot16 == 0, jnp.sum(chosen_f[0:K], axis=0, keepdims=True), count)
        for a in range(1, half):
            lo = K + (a - 1) * half
            count = jnp.where(slot16 == a, jnp.sum(chosen_f[lo:lo + half], axis=0, keepdims=True), count)
        count = jnp.concatenate([count[0:half], chosen_f[K + (half - 1) * half:]], axis=0)
        width = jnp.zeros((N_KEYS, LANES), F32)
        for a in range(K):
            width = jnp.where(rank1 == float(a), count[a:a + 1, :], width)
        in1 = rank1 < float(K)
        in2 = rank2 < float(K)
        p1_ref[h, :, lanes] = jnp.where(in1, jnp.exp(s1 - v1[0:1, :]) / z, 0.0)
        c1_ref[h, :, lanes] = width
        p2_ref[h, :, lanes] = jnp.where(in2, jnp.exp(s2 - v2[0:1, :]), 0.0).astype(p2_ref.dtype)
        r2_ref[h, :, lanes] = rank2.astype(r2_ref.dtype)
        return carry

    lax.fori_loop(0, PEER_HEADS * (tl // LANES), one, 0)


def _peer_select(st, tl):
    T = st.shape[2]
    out_blk = pl.BlockSpec((PEER_HEADS, N_KEYS, tl), lambda i: (0, 0, i))
    shp = jax.ShapeDtypeStruct((PEER_HEADS, N_KEYS, T), F32)
    return pl.pallas_call(
        functools.partial(_peer_select_body, tl=tl),
        grid=(T // tl,),
        in_specs=[pl.BlockSpec((2 * PEER_HEADS, N_KEYS, tl), lambda i: (0, 0, i))],
        out_specs=[out_blk] * 4,
        out_shape=[shp, shp, jax.ShapeDtypeStruct(shp.shape, BF16), jax.ShapeDtypeStruct(shp.shape, BF16)],
        compiler_params=_cparams(("parallel",)),
        name="peer_select",
    )(st)


PEER_SUB = 256


def _peer_body(h2t_ref, p1_ref, c1_ref, p2_in_ref, r2_in_ref, u_ref, vt_ref, o_ref, a_ref, wg_ref, p2_ref, r2_ref,
               *, te):
    e = pl.program_id(1)

    @pl.when(e == 0)
    def _():
        o_ref[...] = jnp.zeros(o_ref.shape, F32)
        p2_ref[...] = p2_in_ref[...]
        r2_ref[...] = r2_in_ref[...]

    tm = o_ref.shape[1]
    a_ref[...] = jnp.dot(u_ref[...], h2t_ref[...], preferred_element_type=F32)
    for j in range(te // N_KEYS):
        rows = slice(j * N_KEYS, (j + 1) * N_KEYS)
        for lt in range(tm // LANES):
            lanes = slice(lt * LANES, (lt + 1) * LANES)
            tile = (N_KEYS, LANES)
            w = jnp.zeros(tile, BF16)
            for h in range(PEER_HEADS):
                width = jnp.broadcast_to(c1_ref[h, j:j + 1, lanes], tile).astype(BF16)
                pfirst = jnp.broadcast_to(p1_ref[h, j:j + 1, lanes], tile).astype(BF16)
                keep = r2_ref[h, :, lanes] < width
                w = w + jnp.where(keep, p2_ref[h, :, lanes], jnp.zeros(tile, BF16)) * pfirst
            act = jax.nn.gelu(a_ref[rows, lanes], approximate=True).astype(BF16)
            wg_ref[rows, lanes] = w * act
    o_ref[...] += jnp.dot(vt_ref[...], wg_ref[...], preferred_element_type=F32)


def _peer_experts(h2t, p1, c1, p2, r2, u_bf16, vt_bf16, tm, te):
    T = h2t.shape[1]
    firsts = te // N_KEYS
    sel_all = pl.BlockSpec((PEER_HEADS, N_KEYS, tm), lambda i, e: (0, 0, i))
    sel_blk = pl.BlockSpec((PEER_HEADS, firsts, tm), lambda i, e: (0, e, i))
    return pl.pallas_call(
        functools.partial(_peer_body, te=te),
        grid=(T // tm, N_EXPERTS // te),
        in_specs=[pl.BlockSpec((D_MODEL, tm), lambda i, e: (0, i)), sel_blk, sel_blk, sel_all, sel_all,
                  pl.BlockSpec((te, D_MODEL), lambda i, e: (e, 0)),
                  pl.BlockSpec((D_MODEL, te), lambda i, e: (0, e))],
        out_specs=pl.BlockSpec((D_MODEL, tm), lambda i, e: (0, i)),
        out_shape=jax.ShapeDtypeStruct((D_MODEL, T), F32),
        scratch_shapes=[pltpu.VMEM((te, tm), F32), pltpu.VMEM((te, tm), BF16),
                        pltpu.VMEM((PEER_HEADS, N_KEYS, tm), BF16), pltpu.VMEM((PEER_HEADS, N_KEYS, tm), BF16)],
        compiler_params=_cparams(("parallel", "arbitrary")),
        name="peer_experts",
    )(h2t, p1, c1, p2, r2, u_bf16, vt_bf16)


def _final_body(x1_ref, ot_ref, g_ref, y_ref):
    y_ref[...] = _rms(x1_ref[...] + ot_ref[...].T, g_ref[...])


def _final_norm(x1, out_t, g, tm):
    T = x1.shape[0]
    return pl.pallas_call(
        _final_body,
        grid=(T // tm,),
        in_specs=[pl.BlockSpec((tm, D_MODEL), lambda i: (i, 0)), pl.BlockSpec((D_MODEL, tm), lambda i: (0, i)),
                  pl.BlockSpec((1, D_MODEL), lambda i: (0, 0))],
        out_specs=pl.BlockSpec((tm, D_MODEL), lambda i: (i, 0)),
        out_shape=jax.ShapeDtypeStruct((T, D_MODEL), F32),
        compiler_params=_cparams(("parallel",)),
        name="final_norm",
    )(x1, out_t, g)


TOKEN_TILE = 512
PEER_EXPERT_TILE = 2048
CONV_ROWS = 512


def _channel_and_final(attn, c, x, p):
    x1, h2t, st = _out_projection(attn, c, x, p["wo"], p["g2"], p["wq"], p["keys"], TOKEN_TILE)
    p1, c1, p2, r2 = _peer_select(st, TOKEN_TILE)
    out_t = _peer_experts(h2t, p1, c1, p2, r2, p["u"], p["vt"], TOKEN_TILE, PEER_EXPERT_TILE)
    return _final_norm(x1, out_t, p["gf"], TOKEN_TILE)


def kernel(x_prompt, x_sample, cache_k, cache_v, state_conv, norm1_g, w_in, conv_w, conv_b, conv_ln_g,
           conv_ln_b, w_out, norm2_g, w_query, sub_keys1, sub_keys2, expert_u, expert_v, final_g):
    bp, s, _ = x_prompt.shape
    bs, t, _ = x_sample.shape
    depth, _, w_buf = cache_k.shape[:3]
    assert depth == 1 and all(s % (d * BAND_BLOCK) == 0 and s // d >= 2 * BAND_BLOCK for _, d in DILATIONS)
    keep = min(DILATIONS[-1][0], s)
    l = 0
    row = lambda a: a.reshape(1, -1)
    p = dict(
        wo=w_out[l].astype(BF16), g2=row(norm2_g[l]), wq=w_query[l].astype(BF16),
        keys=jnp.stack([sub_keys1[l], sub_keys2[l]]).astype(BF16),
        u=expert_u[l].astype(BF16), vt=expert_v[l].T.astype(BF16), gf=row(final_g),
    )
    w_in_b = w_in[l].astype(BF16)
    g1 = row(norm1_g[l])
    cw, cb, lg, lb = conv_w[l], row(conv_b[l]), row(conv_ln_g[l]), row(conv_ln_b[l])

    xp = x_prompt.reshape(bp * s, D_MODEL)
    cos, sin = _rope_tables(jnp.tile(jnp.arange(s), bp))
    q, k, v, g = _in_projection(xp, g1, w_in_b, cos, sin, TOKEN_TILE)
    attn = _attention_prompt(q, k, v, bp, s)
    c = _conv_prompt(g, cw, cb, lg, lb, bp, s, CONV_ROWS)
    y_prompt = _channel_and_final(attn, c, xp, p).reshape(bp, s, D_MODEL)
    new_k_prompt = k.reshape(bp, s, N_HEADS, HEAD_DIM)[None, :, s - keep:]
    new_v_prompt = v.reshape(bp, s, N_HEADS, HEAD_DIM)[None, :, s - keep:]
    new_conv_prompt = g.reshape(bp, s, CONV_CH)[None, :, s - CONV_HIST:]

    xs = x_sample.reshape(bs * t, D_MODEL)
    assert w_buf == min(MAX_WINDOW, PAST_LEN)
    cos, sin = _rope_tables(jnp.tile(PAST_LEN + jnp.arange(t), bs))
    q, k, v, g = _in_projection(xs, g1, w_in_b, cos, sin, TOKEN_TILE)
    heads = lambda a: a.reshape(bs, t, N_HEADS, HEAD_DIM)
    attn = _attention_sample(heads(q), heads(k), heads(v), cache_k[l], cache_v[l])
    attn = attn.reshape(bs * t, ATTN_WIDTH).astype(BF16)
    g_hist = jnp.concatenate([state_conv[l], g.reshape(bs, t, CONV_CH)], axis=1)
    c = _conv_sample(jnp.transpose(g_hist, (1, 0, 2)), cw, cb, lg, lb)
    c = jnp.transpose(c, (1, 0, 2)).reshape(bs * t, CONV_CH)
    y_sample = _channel_and_final(attn, c, xs, p).reshape(bs, t, D_MODEL)
    new_k_sample = k.reshape(bs, t, N_HEADS, HEAD_DIM)[None]
    new_v_sample = v.reshape(bs, t, N_HEADS, HEAD_DIM)[None]
    new_conv_sample = g_hist[None, :, -CONV_HIST:]

    return (y_prompt, y_sample, new_k_prompt, new_v_prompt, new_conv_prompt,
            new_k_sample, new_v_sample, new_conv_sample)
```

```python
import functools

import jax
import jax.numpy as jnp
from jax import lax
from jax.experimental import pallas as pl
from jax.experimental.pallas import tpu as pltpu

F32 = jnp.float32
BF16 = jnp.bfloat16

D_MODEL = 1024
HEAD_DIM = 64
HALF_HEAD = HEAD_DIM // 2
N_HEADS = 12
ATTN_WIDTH = N_HEADS * HEAD_DIM
CONV_CH = D_MODEL - ATTN_WIDTH
CONV_WIDTH = 31
CONV_HIST = CONV_WIDTH - 1
IN_COLS = 3 * ATTN_WIDTH + 2 * CONV_CH
DILATIONS = ((128, 1), (512, 4), (2048, 16))
MAX_WINDOW = 2048
PAST_LEN = 2048
BAND_BLOCK = 128
ROPE_THETA = 10000.0
ATTN_SCALE = HEAD_DIM ** -0.5
NEG_INF = -1e30
N_KEYS = 128
N_EXPERTS = N_KEYS * N_KEYS
PEER_HEADS = 8
PEER_TOPK = 16
D_QUERY = 256
HALF_Q = D_QUERY // 2
RMS_EPS = 1e-6
LN_EPS = 1e-5

LANES = 128
HEADS_PER_LANE_TILE = LANES // HEAD_DIM
N_HEAD_TILES = ATTN_WIDTH // LANES
NOT_SELECTED = 99.0
VMEM_LIMIT = 56 * 1024 * 1024


def _cparams(semantics, flags=None):
    return pltpu.CompilerParams(dimension_semantics=semantics, vmem_limit_bytes=VMEM_LIMIT, flags=flags)


def _rms(x, g):
    return (x * lax.rsqrt(jnp.mean(x * x, axis=-1, keepdims=True) + RMS_EPS)) * g


def _inproj_body(x_ref, g_ref, w_ref, cos_ref, sin_ref, q_ref, k_ref, v_ref, gl_ref):
    hb = _rms(x_ref[...], g_ref[...]).astype(BF16)
    cos = cos_ref[...]
    sin = sin_ref[...]
    lane = lax.broadcasted_iota(jnp.int32, cos.shape, 1)
    first_half = jnp.bitwise_and(lane, HEAD_DIM - 1) < HALF_HEAD

    def rope(t):
        partner = jnp.where(first_half, pltpu.roll(t, LANES - HALF_HEAD, 1), pltpu.roll(t, HALF_HEAD, 1))
        return t * cos + partner * sin

    q = jnp.dot(hb, w_ref[:, 0:ATTN_WIDTH], preferred_element_type=F32)
    k = jnp.dot(hb, w_ref[:, ATTN_WIDTH:2 * ATTN_WIDTH], preferred_element_type=F32)
    for c in range(N_HEAD_TILES):
        sl = slice(c * LANES, (c + 1) * LANES)
        q_ref[:, sl] = rope(q[:, sl]) * ATTN_SCALE
        k_ref[:, sl] = rope(k[:, sl])
    v_ref[...] = jnp.dot(hb, w_ref[:, 2 * ATTN_WIDTH:3 * ATTN_WIDTH], preferred_element_type=F32)
    u = jnp.dot(hb, w_ref[:, 3 * ATTN_WIDTH:IN_COLS], preferred_element_type=F32)
    gl_ref[...] = u[:, :CONV_CH] * jax.nn.sigmoid(u[:, CONV_CH:])


def _in_projection(x, g, w_bf16, cos, sin, tm):
    T = x.shape[0]
    tok = lambda i: (i, 0)
    full = lambda i: (0, 0)
    return pl.pallas_call(
        _inproj_body,
        grid=(T // tm,),
        in_specs=[pl.BlockSpec((tm, D_MODEL), tok), pl.BlockSpec((1, D_MODEL), full),
                  pl.BlockSpec((D_MODEL, IN_COLS), full),
                  pl.BlockSpec((tm, LANES), tok), pl.BlockSpec((tm, LANES), tok)],
        out_specs=[pl.BlockSpec((tm, ATTN_WIDTH), tok)] * 3 + [pl.BlockSpec((tm, CONV_CH), tok)],
        out_shape=[jax.ShapeDtypeStruct((T, ATTN_WIDTH), F32)] * 3 + [jax.ShapeDtypeStruct((T, CONV_CH), F32)],
        compiler_params=_cparams(("parallel",)),
        name="in_projection",
    )(x, g, w_bf16, cos, sin)


def _rope_tables(pos):
    inv = jnp.power(ROPE_THETA, -jnp.arange(HALF_HEAD, dtype=F32) * (2.0 / HEAD_DIM))
    ang = pos.astype(F32)[:, None] * inv[None, :]
    cos, sin = jnp.cos(ang), jnp.sin(ang)
    return (jnp.concatenate([cos, cos] * HEADS_PER_LANE_TILE, axis=-1),
            jnp.concatenate([-sin, sin] * HEADS_PER_LANE_TILE, axis=-1))


ATTN_MERGE_ROWS = 256


def _attn_prompt_body(q_ref, k_ref, v_ref, o_ref, ob_ref, mb_ref, lb_ref, *, seq):
    Q = BAND_BLOCK
    lane = lax.broadcasted_iota(jnp.int32, (Q, LANES), 1)
    head_lanes = [lane < HEAD_DIM, lane >= HEAD_DIM]
    qi = lax.broadcasted_iota(jnp.int32, (Q, 2 * Q), 0)
    ki = lax.broadcasted_iota(jnp.int32, (Q, 2 * Q), 1)

    for b, (win, dil) in enumerate(DILATIONS):
        reach = win // dil
        n_blocks = (seq // dil) // Q

        def rows(start, size, dil=dil):
            return pl.ds(start, size) if dil == 1 else pl.ds(start, size, stride=dil)

        def step(idx, carry, b=b, dil=dil, reach=reach, n_blocks=n_blocks, rows=rows):
            r = idx // n_blocks
            a0 = (idx % n_blocks) * Q
            ks = jnp.maximum(a0 - Q, 0)
            q_rows = rows(r + a0 * dil, Q)
            k_rows = rows(r + ks * dil, 2 * Q)
            q = q_ref[q_rows, :]
            k = k_ref[k_rows, :].astype(BF16)
            v = v_ref[k_rows, :].astype(BF16)
            dist = (a0 - ks) + qi - ki
            valid = (dist >= 0) & (dist <= reach)
            o_full = jnp.zeros((Q, LANES), F32)
            m_full = jnp.zeros((Q, LANES), F32)
            l_full = jnp.zeros((Q, LANES), F32)
            for h in range(HEADS_PER_LANE_TILE):
                qh = jnp.where(head_lanes[h], q, 0.0).astype(BF16)
                s = lax.dot_general(qh, k, (((1,), (1,)), ((), ())), preferred_element_type=F32)
                s = jnp.where(valid, s, NEG_INF)
                m = jnp.max(s, axis=1, keepdims=True)
                p = jnp.exp(s - m)
                l = jnp.sum(p, axis=1, keepdims=True)
                pv = jnp.dot(p.astype(BF16), v, preferred_element_type=F32)
                o_full = jnp.where(head_lanes[h], pv, o_full)
                m_full = jnp.where(head_lanes[h], m, m_full)
                l_full = jnp.where(head_lanes[h], l, l_full)
            ob_ref[b, q_rows, :] = o_full
            mb_ref[b, q_rows, :] = m_full
            lb_ref[b, q_rows, :] = l_full
            return carry

        lax.fori_loop(0, dil * n_blocks, step, 0, unroll=8)

    def merge(i, carry):
        rws = pl.ds(pl.multiple_of(i * ATTN_MERGE_ROWS, ATTN_MERGE_ROWS), ATTN_MERGE_ROWS)
        n_br = len(DILATIONS)
        m_all = mb_ref[0, rws, :]
        for b in range(1, n_br):
            m_all = jnp.maximum(m_all, mb_ref[b, rws, :])
        num = jnp.zeros((ATTN_MERGE_ROWS, LANES), F32)
        den = jnp.zeros((ATTN_MERGE_ROWS, LANES), F32)
        for b in range(n_br):
            w = jnp.exp(mb_ref[b, rws, :] - m_all)
            num = num + w * ob_ref[b, rws, :]
            den = den + w * lb_ref[b, rws, :]
        o_ref[rws, :] = (num / den).astype(o_ref.dtype)
        return carry

    lax.fori_loop(0, seq // ATTN_MERGE_ROWS, merge, 0)


def _attention_prompt(q, k, v, batch, seq):
    T = q.shape[0]
    blk = pl.BlockSpec((seq, LANES), lambda b, hp: (b, hp))
    return pl.pallas_call(
        functools.partial(_attn_prompt_body, seq=seq),
        grid=(batch, N_HEAD_TILES),
        in_specs=[blk, blk, blk],
        out_specs=blk,
        out_shape=jax.ShapeDtypeStruct((T, ATTN_WIDTH), BF16),
        scratch_shapes=[pltpu.VMEM((len(DILATIONS), seq, LANES), F32)] * 3,
        compiler_params=_cparams(("parallel", "parallel")),
        name="attention_prompt",
    )(q, k, v)


SAMPLE_CHUNK = 128


def _attn_sample_body(q_ref, kn_ref, vn_ref, kt_ref, vt_ref, kg_ref, vg_ref, o_ref, s_ref, p_ref, m_ref, l_ref,
                      acc_ref, *, t_new, tail, past):
    c = pl.program_id(1)
    n_seq = q_ref.shape[-1]
    (w1, d1), (w4, d4), (w16, d16) = DILATIONS
    q = [q_ref[i] for i in range(t_new)]

    @pl.when(c == 0)
    def _():
        for i in range(t_new):
            s = [jnp.sum(q[i] * kn_ref[j], axis=0, keepdims=True) for j in range(i + 1)]
            m = s[0]
            for j in range(1, i + 1):
                m = jnp.maximum(m, s[j])
            l = jnp.zeros((1, n_seq), F32)
            acc = jnp.zeros((HEAD_DIM, n_seq), F32)
            for j in range(i + 1):
                p = (float(len(DILATIONS)) if j == i else 1.0) * jnp.exp(s[j] - m)
                l = l + p
                acc = acc + p * vn_ref[j]
            m_ref[i] = m
            l_ref[i] = l
            acc_ref[i] = acc

    def chunk(k_row_of, v_row_of):
        def score_row(r, carry):
            k_row = k_row_of(r)
            for i in range(t_new):
                s_ref[i, pl.ds(r, 1), :] = jnp.sum(q[i] * k_row, axis=0, keepdims=True)
            return carry

        lax.fori_loop(0, SAMPLE_CHUNK, score_row, 0, unroll=2)

        slot = c * SAMPLE_CHUNK + lax.broadcasted_iota(jnp.int32, (SAMPLE_CHUNK, n_seq), 0)
        in_tail = slot < tail
        cache_row = slot + (past - tail)
        grp = jnp.right_shift(slot - tail, shift)
        grp_row = jnp.bitwise_and(slot - tail, low)
        for i in range(t_new):
            in_d1 = in_tail & (cache_row >= past + i - w1 // d1)
            in_d4 = in_tail & (jnp.bitwise_and(cache_row - i, d4 - 1) == 0) & (cache_row >= past + i - w4)
            in_d16 = (~in_tail) & (grp_row == i) & (grp >= past // d16 - w16 // d16)
            mult = in_d1.astype(F32) + in_d4.astype(F32) + in_d16.astype(F32)
            s = jnp.where(mult > 0, s_ref[i], NEG_INF)
            m_old = m_ref[i]
            m_new = jnp.maximum(m_old, jnp.max(s, axis=0, keepdims=True))
            alpha = jnp.exp(m_old - m_new)
            p = mult * jnp.exp(s - m_new)
            p_ref[i] = p
            l_ref[i] = alpha * l_ref[i] + jnp.sum(p, axis=0, keepdims=True)
            m_ref[i] = m_new
            acc_ref[i] = acc_ref[i] * alpha

        def value_row(r, acc):
            v_row = v_row_of(r)
            return tuple(acc[i] + p_ref[i, pl.ds(r, 1), :] * v_row for i in range(t_new))

        acc = lax.fori_loop(0, SAMPLE_CHUNK, value_row, tuple(acc_ref[i] for i in range(t_new)), unroll=2)
        for i in range(t_new):
            acc_ref[i] = acc[i]

    shift, low = t_new.bit_length() - 1, t_new - 1

    @pl.when(c < tail // SAMPLE_CHUNK)
    def _():
        chunk(lambda r: kt_ref[r], lambda r: vt_ref[r])

    @pl.when(c >= tail // SAMPLE_CHUNK)
    def _():
        chunk(lambda r: kg_ref[jnp.right_shift(r, shift), jnp.bitwise_and(r, low)],
              lambda r: vg_ref[jnp.right_shift(r, shift), jnp.bitwise_and(r, low)])

    @pl.when(c == pl.num_programs(1) - 1)
    def _():
        for i in range(t_new):
            o_ref[i] = acc_ref[i] / l_ref[i]


def _attention_sample(q, k_new, v_new, cache_k, cache_v):
    n_seq, t_new = q.shape[:2]
    past = cache_k.shape[1]
    (w1, d1), (w4, d4), (w16, d16) = DILATIONS
    tail = w4
    assert past % d16 == 0 and past >= w16 and w1 <= tail and t_new <= d4 and t_new & (t_new - 1) == 0
    tail_chunks = tail // SAMPLE_CHUNK
    group_chunks = (past // d16 * t_new) // SAMPLE_CHUNK
    groups_per_chunk = SAMPLE_CHUNK // t_new
    assert tail % SAMPLE_CHUNK == 0 and past % SAMPLE_CHUNK == 0 and (past // d16) % groups_per_chunk == 0
    lanes_last = lambda a: jnp.transpose(a, (1, 2, 3, 0))
    grouped = lambda a: a.reshape((past // d16, d16) + a.shape[1:])
    ck, cv = lanes_last(cache_k), lanes_last(cache_v)
    new_blk = pl.BlockSpec((t_new, None, HEAD_DIM, n_seq), lambda h, c: (0, h, 0, 0))
    tail_blk = pl.BlockSpec((SAMPLE_CHUNK, None, HEAD_DIM, n_seq),
                            lambda h, c: (past // SAMPLE_CHUNK - tail_chunks + jnp.minimum(c, tail_chunks - 1), h, 0, 0))
    group_blk = pl.BlockSpec((groups_per_chunk, t_new, None, HEAD_DIM, n_seq),
                             lambda h, c: (jnp.maximum(c - tail_chunks, 0), 0, h, 0, 0))
    out = pl.pallas_call(
        functools.partial(_attn_sample_body, t_new=t_new, tail=tail, past=past),
        grid=(N_HEADS, tail_chunks + group_chunks),
        in_specs=[new_blk, new_blk, new_blk, tail_blk, tail_blk, group_blk, group_blk],
        out_specs=pl.BlockSpec((None, t_new, HEAD_DIM, n_seq), lambda h, c: (h, 0, 0, 0)),
        out_shape=jax.ShapeDtypeStruct((N_HEADS, t_new, HEAD_DIM, n_seq), F32),
        scratch_shapes=[pltpu.VMEM((t_new, SAMPLE_CHUNK, n_seq), F32), pltpu.VMEM((t_new, SAMPLE_CHUNK, n_seq), F32),
                        pltpu.VMEM((t_new, 1, n_seq), F32), pltpu.VMEM((t_new, 1, n_seq), F32),
                        pltpu.VMEM((t_new, HEAD_DIM, n_seq), F32)],
        compiler_params=_cparams(("parallel", "arbitrary")),
        name="attention_sample",
    )(lanes_last(q), lanes_last(k_new), lanes_last(v_new), ck, cv, grouped(ck), grouped(cv))
    return jnp.transpose(out, (3, 1, 0, 2)).reshape(n_seq, t_new, ATTN_WIDTH)


def _ln_swish(y, lg, lb):
    mu = jnp.mean(y, axis=-1, keepdims=True)
    var = jnp.mean(jnp.square(y - mu), axis=-1, keepdims=True)
    yn = (y - mu) * lax.rsqrt(var + LN_EPS) * lg + lb
    return yn * jax.nn.sigmoid(yn)


def _conv_prompt_body(cur_ref, prev_ref, cw_ref, cb_ref, lg_ref, lb_ref, o_ref, win_ref, *, ch):
    j = pl.program_id(1)
    win_ref[0:ch, :] = jnp.where(j == 0, 0.0, prev_ref[...])
    win_ref[ch:2 * ch, :] = cur_ref[...]
    acc = jnp.zeros((ch, CONV_CH), F32)
    for w in range(CONV_WIDTH):
        acc = acc + win_ref[ch - CONV_HIST + w:2 * ch - CONV_HIST + w, :] * cw_ref[w:w + 1, :]
    o_ref[...] = _ln_swish(acc + cb_ref[...], lg_ref[...], lb_ref[...]).astype(o_ref.dtype)


def _conv_prompt(g, cw, cb, lg, lb, batch, seq, ch):
    nj = seq // ch
    par = pl.BlockSpec((1, CONV_CH), lambda b, j: (0, 0))
    return pl.pallas_call(
        functools.partial(_conv_prompt_body, ch=ch),
        grid=(batch, nj),
        in_specs=[pl.BlockSpec((ch, CONV_CH), lambda b, j: (b * nj + j, 0)),
                  pl.BlockSpec((ch, CONV_CH), lambda b, j: (b * nj + jnp.maximum(j - 1, 0), 0)),
                  pl.BlockSpec((CONV_WIDTH, CONV_CH), lambda b, j: (0, 0)), par, par, par],
        out_specs=pl.BlockSpec((ch, CONV_CH), lambda b, j: (b * nj + j, 0)),
        out_shape=jax.ShapeDtypeStruct((batch * seq, CONV_CH), BF16),
        scratch_shapes=[pltpu.VMEM((2 * ch, CONV_CH), F32)],
        compiler_params=_cparams(("parallel", "parallel")),
        name="conv_prompt",
    )(g, g, cw, cb, lg, lb)


def _conv_sample_body(gh_ref, cw_ref, cb_ref, lg_ref, lb_ref, o_ref, *, t_new):
    for i in range(t_new):
        acc = jnp.zeros(gh_ref.shape[1:], F32)
        for w in range(CONV_WIDTH):
            acc = acc + gh_ref[i + w] * cw_ref[w:w + 1, :]
        o_ref[i] = _ln_swish(acc + cb_ref[...], lg_ref[...], lb_ref[...]).astype(o_ref.dtype)


def _conv_sample(gh_t, cw, cb, lg, lb):
    lh, n_seq, _ = gh_t.shape
    t_new = lh - CONV_HIST
    return pl.pallas_call(
        functools.partial(_conv_sample_body, t_new=t_new),
        out_shape=jax.ShapeDtypeStruct((t_new, n_seq, CONV_CH), BF16),
        name="conv_sample",
    )(gh_t, cw, cb, lg, lb)


def _outproj_body(attn_ref, c_ref, x_ref, wo_ref, g2_ref, wq_ref, keys_ref, x1_ref, h2t_ref, st_ref):
    a = jnp.dot(attn_ref[...], wo_ref[0:ATTN_WIDTH, :], preferred_element_type=F32)
    a = a + jnp.dot(c_ref[...], wo_ref[ATTN_WIDTH:D_MODEL, :], preferred_element_type=F32)
    x1 = x_ref[...] + a
    x1_ref[...] = x1
    h2 = _rms(x1, g2_ref[...])
    h2t_ref[...] = h2.T.astype(BF16)
    qv = jnp.dot(h2.astype(BF16), wq_ref[...], preferred_element_type=F32).astype(BF16)
    nt = (((1,), (1,)), ((), ()))
    for h in range(PEER_HEADS):
        for side in range(2):
            qs = qv[:, h * D_QUERY + side * HALF_Q:h * D_QUERY + (side + 1) * HALF_Q]
            st_ref[2 * h + side] = lax.dot_general(keys_ref[side], qs, nt, preferred_element_type=F32)


def _out_projection(attn, c, x, wo_bf16, g2, wq_bf16, keys_bf16, tm):
    T = x.shape[0]
    tok = lambda i: (i, 0)
    full2 = lambda i: (0, 0)
    return pl.pallas_call(
        _outproj_body,
        grid=(T // tm,),
        in_specs=[pl.BlockSpec((tm, ATTN_WIDTH), tok), pl.BlockSpec((tm, CONV_CH), tok),
                  pl.BlockSpec((tm, D_MODEL), tok), pl.BlockSpec((D_MODEL, D_MODEL), full2),
                  pl.BlockSpec((1, D_MODEL), full2), pl.BlockSpec((D_MODEL, PEER_HEADS * D_QUERY), full2),
                  pl.BlockSpec((2, N_KEYS, HALF_Q), lambda i: (0, 0, 0))],
        out_specs=[pl.BlockSpec((tm, D_MODEL), tok), pl.BlockSpec((D_MODEL, tm), lambda i: (0, i)),
                   pl.BlockSpec((2 * PEER_HEADS, N_KEYS, tm), lambda i: (0, 0, i))],
        out_shape=[jax.ShapeDtypeStruct((T, D_MODEL), F32), jax.ShapeDtypeStruct((D_MODEL, T), BF16),
                   jax.ShapeDtypeStruct((2 * PEER_HEADS, N_KEYS, T), F32)],
        compiler_params=_cparams(("parallel",)),
        name="out_projection",
    )(attn, c, x, wo_bf16, g2, wq_bf16, keys_bf16)


def _top16(s, order):
    rank = jnp.full(s.shape, NOT_SELECTED, F32)
    slot = lax.broadcasted_iota(jnp.int32, (PEER_TOPK, s.shape[1]), 0)
    vals = jnp.zeros((PEER_TOPK, s.shape[1]), F32)
    for a in range(PEER_TOPK):
        mx = jnp.max(s, axis=0, keepdims=True)
        first = jnp.min(jnp.where(s == mx, order, 1e9), axis=0, keepdims=True)
        sel = order == first
        rank = jnp.where(sel, float(a), rank)
        s = jnp.where(sel, -jnp.inf, s)
        vals = jnp.where(slot == a, mx, vals)
    return rank, vals


def _cex(a, b):
    if a is None or b is None:
        return (b if a is None else a), None
    return jnp.maximum(a, b), jnp.minimum(a, b)


def _bitonic_sort_desc(xs):
    xs, n, k = list(xs), len(xs), 2
    while k <= n:
        j = k // 2
        while j >= 1:
            for i in range(n):
                if i ^ j > i:
                    hi, lo = _cex(xs[i], xs[i ^ j])
                    xs[i], xs[i ^ j] = (hi, lo) if (i & k) == 0 else (lo, hi)
            j //= 2
        k *= 2
    return xs


def _bitonic_merge_desc(xs):
    xs, j = list(xs), len(xs) // 2
    while j >= 1:
        for i in range(len(xs)):
            if (i & j) == 0:
                xs[i], xs[i + j] = _cex(xs[i], xs[i + j])
        j //= 2
    return xs


def _top16_values(slabs):
    K = PEER_TOPK
    xs = _bitonic_sort_desc(list(slabs) + [None] * (K - len(slabs)))
    for shift in (4, 2, 1):
        other = [None if x is None else pltpu.roll(x, shift, 0) for x in xs]
        xs = _bitonic_merge_desc([_cex(xs[i], other[K - 1 - i])[0] for i in range(K)])
    return xs


def _search16(test, v):
    sel = jnp.where
    c1 = test(v[7])
    c2 = test(sel(c1, v[11], v[3]))
    c3 = test(sel(c1, sel(c2, v[13], v[9]), sel(c2, v[5], v[1])))
    c4 = test(sel(c1, sel(c2, sel(c3, v[14], v[12]), sel(c3, v[10], v[8])),
                  sel(c2, sel(c3, v[6], v[4]), sel(c3, v[2], v[0]))))
    return sel(c1, 8.0, 0.0) + sel(c2, 4.0, 0.0) + sel(c3, 2.0, 0.0) + sel(c4, 1.0, 0.0)


def _rows_sum(x):
    for shift in (4, 2, 1):
        x = x + pltpu.roll(x, shift, 0)
    return x


def _select_fast(s1, s2):
    K, R = PEER_TOPK, 8
    x1 = [s1[R * i:R * (i + 1)] for i in range(N_KEYS // R)]
    x2 = [s2[R * i:R * (i + 1)] for i in range(N_KEYS // R)]
    v1, v2 = _top16_values(x1), _top16_values(x2)
    row = lax.broadcasted_iota(jnp.int32, x1[0].shape, 0)

    def rows_of(vals):
        out = vals[0]
        for b in range(1, R):
            out = jnp.where(row == b, vals[b], out)
        return out

    v2_lo, v2_hi, v1_hi = rows_of(v2[0:R]), rows_of(v2[R:K]), rows_of(v1[R:K])
    cand = [v1[0] + v2_lo, v1[0] + v2_hi] + [v1[a] + v2_lo for a in range(1, R)] + [v1_hi + v2[0]]
    cs = _top16_values(cand)
    tau = cs[K - 1]
    z = jnp.exp(cs[0] - cs[0])
    for a in range(1, K):
        z = z + jnp.exp(cs[a] - cs[0])

    def count_ge(xs, t):
        n = jnp.zeros(xs[0].shape, F32)
        for x in xs:
            n = n + jnp.where(x >= t, 1.0, 0.0)
        return _rows_sum(n)

    tie = (count_ge(x1, v1[K - 1]) != K) | (count_ge(x2, v2[K - 1]) != K) | (count_ge(cand, tau) != K)
    for a in range(K - 1):
        tie = tie | (v1[a] == v1[a + 1]) | (v2[a] == v2[a + 1])

    p1, width, p2, r2 = [], [], [], []
    for x in x1:
        in1 = x >= v1[K - 1]
        wid = _search16(lambda vb, x=x: (x + vb) >= tau, v2) + jnp.where((x + v2[K - 1]) >= tau, 1.0, 0.0)
        p1.append(jnp.where(in1, jnp.exp(x - v1[0]) / z, 0.0))
        width.append(jnp.where(in1, wid, 0.0))
    for x in x2:
        in2 = x >= v2[K - 1]
        p2.append(jnp.where(in2, jnp.exp(x - v2[0]), 0.0))
        r2.append(jnp.where(in2, _search16(lambda vb, x=x: vb > x, v2), NOT_SELECTED))
    cat = lambda parts: jnp.concatenate(parts, axis=0)
    return cat(p1), cat(width), cat(p2), cat(r2), tie


def _select_exact(s1, s2, key_order, cand_order):
    K = PEER_TOPK
    half = K // 2
    slot16 = lax.broadcasted_iota(jnp.int32, (K, s1.shape[1]), 0)
    rank1, v1 = _top16(s1, key_order)
    rank2, v2 = _top16(s2, key_order)
    cand = [v1[0:1, :] + v2]
    cand += [v1[a:a + 1, :] + v2[0:half, :] for a in range(1, half)]
    cand += [v1[half:K, :] + v2[0:1, :]]
    cand = jnp.concatenate(cand, axis=0)
    crank, _ = _top16(cand, cand_order)
    chosen = crank < float(K)
    top = v1[0:1, :] + v2[0:1, :]
    z = jnp.sum(jnp.where(chosen, jnp.exp(cand - top), 0.0), axis=0, keepdims=True)
    chosen_f = chosen.astype(F32)
    count = jnp.zeros((K, s1.shape[1]), F32)
    count = jnp.where(slot16 == 0, jnp.sum(chosen_f[0:K], axis=0, keepdims=True), count)
    for a in range(1, half):
        lo = K + (a - 1) * half
        count = jnp.where(slot16 == a, jnp.sum(chosen_f[lo:lo + half], axis=0, keepdims=True), count)
    count = jnp.concatenate([count[0:half], chosen_f[K + (half - 1) * half:]], axis=0)
    width = jnp.zeros(s1.shape, F32)
    for a in range(K):
        width = jnp.where(rank1 == float(a), count[a:a + 1, :], width)
    in1 = rank1 < float(K)
    in2 = rank2 < float(K)
    return (jnp.where(in1, jnp.exp(s1 - v1[0:1, :]) / z, 0.0), width,
            jnp.where(in2, jnp.exp(s2 - v2[0:1, :]), 0.0), rank2)


def _peer_select_body(st_ref, p1_ref, c1_ref, p2_ref, r2_ref, *, tl):
    K = PEER_TOPK
    half = K // 2
    key_order = lax.broadcasted_iota(jnp.int32, (N_KEYS, LANES), 0).astype(F32)
    sub = lax.broadcasted_iota(jnp.int32, (half, LANES), 0)
    flat = [lax.broadcasted_iota(jnp.int32, (K, LANES), 0)]
    flat += [a * K + sub for a in range(1, half)]
    flat += [(half + sub) * K]
    cand_order = jnp.concatenate(flat, axis=0).astype(F32)

    def one(idx, carry):
        h = idx // (tl // LANES)
        ls = pl.multiple_of((idx % (tl // LANES)) * LANES, LANES)
        lanes = pl.ds(ls, LANES)
        s1 = st_ref[2 * h, :, lanes]
        s2 = st_ref[2 * h + 1, :, lanes]

        def write(p1, width, p2, r2):
            p1_ref[h, :, lanes] = p1
            c1_ref[h, :, lanes] = width
            p2_ref[h, :, lanes] = p2.astype(p2_ref.dtype)
            r2_ref[h, :, lanes] = r2.astype(r2_ref.dtype)

        *fast, tie = _select_fast(s1, s2)
        write(*fast)

        @pl.when(jnp.max(jnp.where(tie, 1.0, 0.0)) > 0.0)
        def _():
            write(*_select_exact(s1, s2, key_order, cand_order))

        return carry

    lax.fori_loop(0, PEER_HEADS * (tl // LANES), one, 0)


def _peer_select(st, tl):
    T = st.shape[2]
    out_blk = pl.BlockSpec((PEER_HEADS, N_KEYS, tl), lambda i: (0, 0, i))
    shp = jax.ShapeDtypeStruct((PEER_HEADS, N_KEYS, T), F32)
    return pl.pallas_call(
        functools.partial(_peer_select_body, tl=tl),
        grid=(T // tl,),
        in_specs=[pl.BlockSpec((2 * PEER_HEADS, N_KEYS, tl), lambda i: (0, 0, i))],
        out_specs=[out_blk] * 4,
        out_shape=[shp, shp, jax.ShapeDtypeStruct(shp.shape, BF16), jax.ShapeDtypeStruct(shp.shape, BF16)],
        compiler_params=_cparams(("parallel",)),
        name="peer_select",
    )(st)


PEER_CHUNK = 256


def _peer_body(h2t_ref, p1_ref, c1_ref, p2_in_ref, r2_in_ref, u_ref, vt_ref, o_ref, a0_ref, a1_ref, w0_ref, w1_ref,
               p2_ref, r2_ref, *, te):
    e = pl.program_id(1)

    @pl.when(e == 0)
    def _():
        o_ref[...] = jnp.zeros(o_ref.shape, F32)
        p2_ref[...] = p2_in_ref[...]
        r2_ref[...] = r2_in_ref[...]

    tm = o_ref.shape[1]
    n_chunks = te // PEER_CHUNK
    row0 = pl.multiple_of(jnp.minimum(e, 0) * N_KEYS, N_KEYS)
    a_refs, w_refs = (a0_ref, a1_ref), (w0_ref, w1_ref)

    def chunk_rows(c):
        return pl.ds(c * PEER_CHUNK, PEER_CHUNK)

    def project(c, slot):
        a_refs[slot][...] = jnp.dot(u_ref[chunk_rows(c), :], h2t_ref[...], preferred_element_type=F32)

    def gate(c, slot):
        for jj in range(PEER_CHUNK // N_KEYS):
            j = c * (PEER_CHUNK // N_KEYS) + jj
            rows = slice(jj * N_KEYS, (jj + 1) * N_KEYS)
            for lt in range(tm // LANES):
                lanes = slice(lt * LANES, (lt + 1) * LANES)
                tile = (N_KEYS, LANES)
                w = jnp.zeros(tile, BF16)
                for h in range(PEER_HEADS):
                    width = jnp.broadcast_to(c1_ref[h, pl.ds(j, 1), lanes], tile).astype(BF16)
                    pfirst = jnp.broadcast_to(p1_ref[h, pl.ds(j, 1), lanes], tile).astype(BF16)
                    keep = r2_ref[h, :, lanes] < width
                    w = w + jnp.where(keep, p2_ref[h, :, lanes], jnp.zeros(tile, BF16)) * pfirst
                act = jax.nn.gelu(a_refs[slot][pl.ds(row0 + jj * N_KEYS, N_KEYS), lanes], approximate=True)
                act = act.astype(BF16)
                w_refs[slot][rows, lanes] = w * act

    def combine(c, slot):
        out_rows = pl.ds(row0, D_MODEL)
        o_ref[out_rows, :] += jnp.dot(vt_ref[:, chunk_rows(c)], w_refs[slot][...], preferred_element_type=F32)

    project(0, 0)
    gate(0, 0)
    project(1, 1)

    def pair(k, carry):
        c = 2 * k
        combine(c, 0)
        gate(c + 1, 1)
        project(c + 2, 0)
        combine(c + 1, 1)
        gate(c + 2, 0)
        project(c + 3, 1)
        return carry

    for k in range(n_chunks // 2 - 1):
        pair(k, 0)
    combine(n_chunks - 2, 0)
    gate(n_chunks - 1, 1)
    combine(n_chunks - 1, 1)


def _peer_experts(h2t, p1, c1, p2, r2, u_bf16, vt_bf16, tm, te):
    T = h2t.shape[1]
    firsts = te // N_KEYS
    sel_all = pl.BlockSpec((PEER_HEADS, N_KEYS, tm), lambda i, e: (0, 0, i))
    sel_blk = pl.BlockSpec((PEER_HEADS, firsts, tm), lambda i, e: (0, e, i))
    return pl.pallas_call(
        functools.partial(_peer_body, te=te),
        grid=(T // tm, N_EXPERTS // te),
        in_specs=[pl.BlockSpec((D_MODEL, tm), lambda i, e: (0, i)), sel_blk, sel_blk, sel_all, sel_all,
                  pl.BlockSpec((te, D_MODEL), lambda i, e: (e, 0)),
                  pl.BlockSpec((D_MODEL, te), lambda i, e: (0, e))],
        out_specs=pl.BlockSpec((D_MODEL, tm), lambda i, e: (0, i)),
        out_shape=jax.ShapeDtypeStruct((D_MODEL, T), F32),
        scratch_shapes=[pltpu.VMEM((PEER_CHUNK, tm), F32)] * 2 + [pltpu.VMEM((PEER_CHUNK, tm), BF16)] * 2
        + [pltpu.VMEM((PEER_HEADS, N_KEYS, tm), BF16)] * 2,
        compiler_params=_cparams(("parallel", "arbitrary")),
        name="peer_experts",
    )(h2t, p1, c1, p2, r2, u_bf16, vt_bf16)


def _final_body(x1_ref, ot_ref, g_ref, y_ref):
    y_ref[...] = _rms(x1_ref[...] + ot_ref[...].T, g_ref[...])


def _final_norm(x1, out_t, g, tm):
    T = x1.shape[0]
    return pl.pallas_call(
        _final_body,
        grid=(T // tm,),
        in_specs=[pl.BlockSpec((tm, D_MODEL), lambda i: (i, 0)), pl.BlockSpec((D_MODEL, tm), lambda i: (0, i)),
                  pl.BlockSpec((1, D_MODEL), lambda i: (0, 0))],
        out_specs=pl.BlockSpec((tm, D_MODEL), lambda i: (i, 0)),
        out_shape=jax.ShapeDtypeStruct((T, D_MODEL), F32),
        compiler_params=_cparams(("parallel",)),
        name="final_norm",
    )(x1, out_t, g)


TOKEN_TILE = 512
PEER_EXPERT_TILE = 2048
CONV_ROWS = 512


def _channel_and_final(attn, c, x, p):
    x1, h2t, st = _out_projection(attn, c, x, p["wo"], p["g2"], p["wq"], p["keys"], TOKEN_TILE)
    p1, c1, p2, r2 = _peer_select(st, TOKEN_TILE)
    out_t = _peer_experts(h2t, p1, c1, p2, r2, p["u"], p["vt"], TOKEN_TILE, PEER_EXPERT_TILE)
    return _final_norm(x1, out_t, p["gf"], TOKEN_TILE)


def kernel(x_prompt, x_sample, cache_k, cache_v, state_conv, norm1_g, w_in, conv_w, conv_b, conv_ln_g,
           conv_ln_b, w_out, norm2_g, w_query, sub_keys1, sub_keys2, expert_u, expert_v, final_g):
    bp, s, _ = x_prompt.shape
    bs, t, _ = x_sample.shape
    depth, _, w_buf = cache_k.shape[:3]
    assert depth == 1 and all(s % (d * BAND_BLOCK) == 0 and s // d >= 2 * BAND_BLOCK for _, d in DILATIONS)
    keep = min(DILATIONS[-1][0], s)
    l = 0
    row = lambda a: a.reshape(1, -1)
    p = dict(
        wo=w_out[l].astype(BF16), g2=row(norm2_g[l]), wq=w_query[l].astype(BF16),
        keys=jnp.stack([sub_keys1[l], sub_keys2[l]]).astype(BF16),
        u=expert_u[l].astype(BF16), vt=expert_v[l].T.astype(BF16), gf=row(final_g),
    )
    w_in_b = w_in[l].astype(BF16)
    g1 = row(norm1_g[l])
    cw, cb, lg, lb = conv_w[l], row(conv_b[l]), row(conv_ln_g[l]), row(conv_ln_b[l])

    xp = x_prompt.reshape(bp * s, D_MODEL)
    cos, sin = _rope_tables(jnp.tile(jnp.arange(s), bp))
    q, k, v, g = _in_projection(xp, g1, w_in_b, cos, sin, TOKEN_TILE)
    attn = _attention_prompt(q, k, v, bp, s)
    c = _conv_prompt(g, cw, cb, lg, lb, bp, s, CONV_ROWS)
    y_prompt = _channel_and_final(attn, c, xp, p).reshape(bp, s, D_MODEL)
    new_k_prompt = k.reshape(bp, s, N_HEADS, HEAD_DIM)[None, :, s - keep:]
    new_v_prompt = v.reshape(bp, s, N_HEADS, HEAD_DIM)[None, :, s - keep:]
    new_conv_prompt = g.reshape(bp, s, CONV_CH)[None, :, s - CONV_HIST:]

    xs = x_sample.reshape(bs * t, D_MODEL)
    assert w_buf == min(MAX_WINDOW, PAST_LEN)
    cos, sin = _rope_tables(jnp.tile(PAST_LEN + jnp.arange(t), bs))
    q, k, v, g = _in_projection(xs, g1, w_in_b, cos, sin, TOKEN_TILE)
    heads = lambda a: a.reshape(bs, t, N_HEADS, HEAD_DIM)
    attn = _attention_sample(heads(q), heads(k), heads(v), cache_k[l], cache_v[l])
    attn = attn.reshape(bs * t, ATTN_WIDTH).astype(BF16)
    g_hist = jnp.concatenate([state_conv[l], g.reshape(bs, t, CONV_CH)], axis=1)
    c = _conv_sample(jnp.transpose(g_hist, (1, 0, 2)), cw, cb, lg, lb)
    c = jnp.transpose(c, (1, 0, 2)).reshape(bs * t, CONV_CH)
    y_sample = _channel_and_final(attn, c, xs, p).reshape(bs, t, D_MODEL)
    new_k_sample = k.reshape(bs, t, N_HEADS, HEAD_DIM)[None]
    new_v_sample = v.reshape(bs, t, N_HEADS, HEAD_DIM)[None]
    new_conv_sample = g_hist[None, :, -CONV_HIST:]

    return (y_prompt, y_sample, new_k_prompt, new_v_prompt, new_conv_prompt,
            new_k_sample, new_v_sample, new_conv_sample)
```

```python
import functools

import jax
import jax.numpy as jnp
from jax import lax
from jax.experimental import pallas as pl
from jax.experimental.pallas import tpu as pltpu

F32 = jnp.float32
BF16 = jnp.bfloat16

D_MODEL = 1024
HEAD_DIM = 64
HALF_HEAD = HEAD_DIM // 2
N_HEADS = 12
ATTN_WIDTH = N_HEADS * HEAD_DIM
CONV_CH = D_MODEL - ATTN_WIDTH
CONV_WIDTH = 31
CONV_HIST = CONV_WIDTH - 1
IN_COLS = 3 * ATTN_WIDTH + 2 * CONV_CH
DILATIONS = ((128, 1), (512, 4), (2048, 16))
MAX_WINDOW = 2048
PAST_LEN = 2048
BAND_BLOCK = 128
ROPE_THETA = 10000.0
ATTN_SCALE = HEAD_DIM ** -0.5
NEG_INF = -1e30
N_KEYS = 128
N_EXPERTS = N_KEYS * N_KEYS
PEER_HEADS = 8
PEER_TOPK = 16
D_QUERY = 256
HALF_Q = D_QUERY // 2
RMS_EPS = 1e-6
LN_EPS = 1e-5

LANES = 128
HEADS_PER_LANE_TILE = LANES // HEAD_DIM
N_HEAD_TILES = ATTN_WIDTH // LANES
NOT_SELECTED = 99.0
VMEM_LIMIT = 56 * 1024 * 1024


def _cparams(semantics, flags=None):
    return pltpu.CompilerParams(dimension_semantics=semantics, vmem_limit_bytes=VMEM_LIMIT, flags=flags)


def _rms(x, g):
    return (x * lax.rsqrt(jnp.mean(x * x, axis=-1, keepdims=True) + RMS_EPS)) * g


def _inproj_body(x_ref, g_ref, w_ref, cos_ref, sin_ref, q_ref, k_ref, v_ref, gl_ref):
    hb = _rms(x_ref[...], g_ref[...]).astype(BF16)
    cos = cos_ref[...]
    sin = sin_ref[...]
    lane = lax.broadcasted_iota(jnp.int32, cos.shape, 1)
    first_half = jnp.bitwise_and(lane, HEAD_DIM - 1) < HALF_HEAD

    def rope(t):
        partner = jnp.where(first_half, pltpu.roll(t, LANES - HALF_HEAD, 1), pltpu.roll(t, HALF_HEAD, 1))
        return t * cos + partner * sin

    q = jnp.dot(hb, w_ref[:, 0:ATTN_WIDTH], preferred_element_type=F32)
    k = jnp.dot(hb, w_ref[:, ATTN_WIDTH:2 * ATTN_WIDTH], preferred_element_type=F32)
    for c in range(N_HEAD_TILES):
        sl = slice(c * LANES, (c + 1) * LANES)
        q_ref[:, sl] = rope(q[:, sl]) * ATTN_SCALE
        k_ref[:, sl] = rope(k[:, sl])
    v_ref[...] = jnp.dot(hb, w_ref[:, 2 * ATTN_WIDTH:3 * ATTN_WIDTH], preferred_element_type=F32)
    u = jnp.dot(hb, w_ref[:, 3 * ATTN_WIDTH:IN_COLS], preferred_element_type=F32)
    gl_ref[...] = u[:, :CONV_CH] * jax.nn.sigmoid(u[:, CONV_CH:])


def _in_projection(x, g, w_bf16, cos, sin, tm):
    T = x.shape[0]
    tok = lambda i: (i, 0)
    full = lambda i: (0, 0)
    return pl.pallas_call(
        _inproj_body,
        grid=(T // tm,),
        in_specs=[pl.BlockSpec((tm, D_MODEL), tok), pl.BlockSpec((1, D_MODEL), full),
                  pl.BlockSpec((D_MODEL, IN_COLS), full),
                  pl.BlockSpec((tm, LANES), tok), pl.BlockSpec((tm, LANES), tok)],
        out_specs=[pl.BlockSpec((tm, ATTN_WIDTH), tok)] * 3 + [pl.BlockSpec((tm, CONV_CH), tok)],
        out_shape=[jax.ShapeDtypeStruct((T, ATTN_WIDTH), F32)] * 3 + [jax.ShapeDtypeStruct((T, CONV_CH), F32)],
        compiler_params=_cparams(("parallel",)),
        name="in_projection",
    )(x, g, w_bf16, cos, sin)


def _rope_tables(pos):
    inv = jnp.power(ROPE_THETA, -jnp.arange(HALF_HEAD, dtype=F32) * (2.0 / HEAD_DIM))
    ang = pos.astype(F32)[:, None] * inv[None, :]
    cos, sin = jnp.cos(ang), jnp.sin(ang)
    return (jnp.concatenate([cos, cos] * HEADS_PER_LANE_TILE, axis=-1),
            jnp.concatenate([-sin, sin] * HEADS_PER_LANE_TILE, axis=-1))


ATTN_MERGE_ROWS = 256


def _attn_prompt_body(q_ref, k_ref, v_ref, o_ref, ob_ref, mb_ref, lb_ref, *, seq):
    Q = BAND_BLOCK
    lane = lax.broadcasted_iota(jnp.int32, (Q, LANES), 1)
    head_lanes = [lane < HEAD_DIM, lane >= HEAD_DIM]
    qi = lax.broadcasted_iota(jnp.int32, (Q, 2 * Q), 0)
    ki = lax.broadcasted_iota(jnp.int32, (Q, 2 * Q), 1)

    for b, (win, dil) in enumerate(DILATIONS):
        reach = win // dil
        n_blocks = (seq // dil) // Q

        def rows(start, size, dil=dil):
            return pl.ds(start, size) if dil == 1 else pl.ds(start, size, stride=dil)

        def step(idx, carry, b=b, dil=dil, reach=reach, n_blocks=n_blocks, rows=rows):
            r = idx // n_blocks
            a0 = (idx % n_blocks) * Q
            ks = jnp.maximum(a0 - Q, 0)
            q_rows = rows(r + a0 * dil, Q)
            k_rows = rows(r + ks * dil, 2 * Q)
            q = q_ref[q_rows, :]
            k = k_ref[k_rows, :].astype(BF16)
            v = v_ref[k_rows, :].astype(BF16)
            dist = (a0 - ks) + qi - ki
            valid = (dist >= 0) & (dist <= reach)
            o_full = jnp.zeros((Q, LANES), F32)
            m_full = jnp.zeros((Q, LANES), F32)
            l_full = jnp.zeros((Q, LANES), F32)
            for h in range(HEADS_PER_LANE_TILE):
                qh = jnp.where(head_lanes[h], q, 0.0).astype(BF16)
                s = lax.dot_general(qh, k, (((1,), (1,)), ((), ())), preferred_element_type=F32)
                s = jnp.where(valid, s, NEG_INF)
                m = jnp.max(s, axis=1, keepdims=True)
                p = jnp.exp(s - m)
                l = jnp.sum(p, axis=1, keepdims=True)
                pv = jnp.dot(p.astype(BF16), v, preferred_element_type=F32)
                o_full = jnp.where(head_lanes[h], pv, o_full)
                m_full = jnp.where(head_lanes[h], m, m_full)
                l_full = jnp.where(head_lanes[h], l, l_full)
            ob_ref[b, q_rows, :] = o_full
            mb_ref[b, q_rows, :] = m_full
            lb_ref[b, q_rows, :] = l_full
            return carry

        lax.fori_loop(0, dil * n_blocks, step, 0, unroll=8)

    def merge(i, carry):
        rws = pl.ds(pl.multiple_of(i * ATTN_MERGE_ROWS, ATTN_MERGE_ROWS), ATTN_MERGE_ROWS)
        n_br = len(DILATIONS)
        m_all = mb_ref[0, rws, :]
        for b in range(1, n_br):
            m_all = jnp.maximum(m_all, mb_ref[b, rws, :])
        num = jnp.zeros((ATTN_MERGE_ROWS, LANES), F32)
        den = jnp.zeros((ATTN_MERGE_ROWS, LANES), F32)
        for b in range(n_br):
            w = jnp.exp(mb_ref[b, rws, :] - m_all)
            num = num + w * ob_ref[b, rws, :]
            den = den + w * lb_ref[b, rws, :]
        o_ref[rws, :] = (num / den).astype(o_ref.dtype)
        return carry

    lax.fori_loop(0, seq // ATTN_MERGE_ROWS, merge, 0)


def _attention_prompt(q, k, v, batch, seq):
    T = q.shape[0]
    blk = pl.BlockSpec((seq, LANES), lambda b, hp: (b, hp))
    return pl.pallas_call(
        functools.partial(_attn_prompt_body, seq=seq),
        grid=(batch, N_HEAD_TILES),
        in_specs=[blk, blk, blk],
        out_specs=blk,
        out_shape=jax.ShapeDtypeStruct((T, ATTN_WIDTH), BF16),
        scratch_shapes=[pltpu.VMEM((len(DILATIONS), seq, LANES), F32)] * 3,
        compiler_params=_cparams(("parallel", "parallel")),
        name="attention_prompt",
    )(q, k, v)


SAMPLE_CHUNK = 128


def _attn_sample_body(q_ref, kn_ref, vn_ref, kt_ref, vt_ref, kg_ref, vg_ref, o_ref, s_ref, p_ref, m_ref, l_ref,
                      acc_ref, *, t_new, tail, past):
    c = pl.program_id(1)
    n_seq = q_ref.shape[-1]
    (w1, d1), (w4, d4), (w16, d16) = DILATIONS
    q = [q_ref[i] for i in range(t_new)]

    @pl.when(c == 0)
    def _():
        for i in range(t_new):
            s = [jnp.sum(q[i] * kn_ref[j], axis=0, keepdims=True) for j in range(i + 1)]
            m = s[0]
            for j in range(1, i + 1):
                m = jnp.maximum(m, s[j])
            l = jnp.zeros((1, n_seq), F32)
            acc = jnp.zeros((HEAD_DIM, n_seq), F32)
            for j in range(i + 1):
                p = (float(len(DILATIONS)) if j == i else 1.0) * jnp.exp(s[j] - m)
                l = l + p
                acc = acc + p * vn_ref[j]
            m_ref[i] = m
            l_ref[i] = l
            acc_ref[i] = acc

    def chunk(k_row_of, v_row_of):
        def score_row(r, carry):
            k_row = k_row_of(r)
            for i in range(t_new):
                s_ref[i, pl.ds(r, 1), :] = jnp.sum(q[i] * k_row, axis=0, keepdims=True)
            return carry

        lax.fori_loop(0, SAMPLE_CHUNK, score_row, 0, unroll=2)

        slot = c * SAMPLE_CHUNK + lax.broadcasted_iota(jnp.int32, (SAMPLE_CHUNK, n_seq), 0)
        in_tail = slot < tail
        cache_row = slot + (past - tail)
        grp = jnp.right_shift(slot - tail, shift)
        grp_row = jnp.bitwise_and(slot - tail, low)
        for i in range(t_new):
            in_d1 = in_tail & (cache_row >= past + i - w1 // d1)
            in_d4 = in_tail & (jnp.bitwise_and(cache_row - i, d4 - 1) == 0) & (cache_row >= past + i - w4)
            in_d16 = (~in_tail) & (grp_row == i) & (grp >= past // d16 - w16 // d16)
            mult = in_d1.astype(F32) + in_d4.astype(F32) + in_d16.astype(F32)
            s = jnp.where(mult > 0, s_ref[i], NEG_INF)
            m_old = m_ref[i]
            m_new = jnp.maximum(m_old, jnp.max(s, axis=0, keepdims=True))
            alpha = jnp.exp(m_old - m_new)
            p = mult * jnp.exp(s - m_new)
            p_ref[i] = p
            l_ref[i] = alpha * l_ref[i] + jnp.sum(p, axis=0, keepdims=True)
            m_ref[i] = m_new
            acc_ref[i] = acc_ref[i] * alpha

        def value_row(r, acc):
            v_row = v_row_of(r)
            return tuple(acc[i] + p_ref[i, pl.ds(r, 1), :] * v_row for i in range(t_new))

        acc = lax.fori_loop(0, SAMPLE_CHUNK, value_row, tuple(acc_ref[i] for i in range(t_new)), unroll=2)
        for i in range(t_new):
            acc_ref[i] = acc[i]

    shift, low = t_new.bit_length() - 1, t_new - 1

    @pl.when(c < tail // SAMPLE_CHUNK)
    def _():
        chunk(lambda r: kt_ref[r], lambda r: vt_ref[r])

    @pl.when(c >= tail // SAMPLE_CHUNK)
    def _():
        chunk(lambda r: kg_ref[jnp.right_shift(r, shift), jnp.bitwise_and(r, low)],
              lambda r: vg_ref[jnp.right_shift(r, shift), jnp.bitwise_and(r, low)])

    @pl.when(c == pl.num_programs(1) - 1)
    def _():
        for i in range(t_new):
            o_ref[i] = acc_ref[i] / l_ref[i]


def _attention_sample(q, k_new, v_new, cache_k, cache_v):
    n_seq, t_new = q.shape[:2]
    past = cache_k.shape[1]
    (w1, d1), (w4, d4), (w16, d16) = DILATIONS
    tail = w4
    assert past % d16 == 0 and past >= w16 and w1 <= tail and t_new <= d4 and t_new & (t_new - 1) == 0
    tail_chunks = tail // SAMPLE_CHUNK
    group_chunks = (past // d16 * t_new) // SAMPLE_CHUNK
    groups_per_chunk = SAMPLE_CHUNK // t_new
    assert tail % SAMPLE_CHUNK == 0 and past % SAMPLE_CHUNK == 0 and (past // d16) % groups_per_chunk == 0
    lanes_last = lambda a: jnp.transpose(a, (1, 2, 3, 0))
    grouped = lambda a: a.reshape((past // d16, d16) + a.shape[1:])
    ck, cv = lanes_last(cache_k), lanes_last(cache_v)
    new_blk = pl.BlockSpec((t_new, None, HEAD_DIM, n_seq), lambda h, c: (0, h, 0, 0))
    tail_blk = pl.BlockSpec((SAMPLE_CHUNK, None, HEAD_DIM, n_seq),
                            lambda h, c: (past // SAMPLE_CHUNK - tail_chunks + jnp.minimum(c, tail_chunks - 1), h, 0, 0))
    group_blk = pl.BlockSpec((groups_per_chunk, t_new, None, HEAD_DIM, n_seq),
                             lambda h, c: (jnp.maximum(c - tail_chunks, 0), 0, h, 0, 0))
    out = pl.pallas_call(
        functools.partial(_attn_sample_body, t_new=t_new, tail=tail, past=past),
        grid=(N_HEADS, tail_chunks + group_chunks),
        in_specs=[new_blk, new_blk, new_blk, tail_blk, tail_blk, group_blk, group_blk],
        out_specs=pl.BlockSpec((None, t_new, HEAD_DIM, n_seq), lambda h, c: (h, 0, 0, 0)),
        out_shape=jax.ShapeDtypeStruct((N_HEADS, t_new, HEAD_DIM, n_seq), F32),
        scratch_shapes=[pltpu.VMEM((t_new, SAMPLE_CHUNK, n_seq), F32), pltpu.VMEM((t_new, SAMPLE_CHUNK, n_seq), F32),
                        pltpu.VMEM((t_new, 1, n_seq), F32), pltpu.VMEM((t_new, 1, n_seq), F32),
                        pltpu.VMEM((t_new, HEAD_DIM, n_seq), F32)],
        compiler_params=_cparams(("parallel", "arbitrary")),
        name="attention_sample",
    )(lanes_last(q), lanes_last(k_new), lanes_last(v_new), ck, cv, grouped(ck), grouped(cv))
    return jnp.transpose(out, (3, 1, 0, 2)).reshape(n_seq, t_new, ATTN_WIDTH)


def _ln_swish(y, lg, lb):
    mu = jnp.mean(y, axis=-1, keepdims=True)
    var = jnp.mean(jnp.square(y - mu), axis=-1, keepdims=True)
    yn = (y - mu) * lax.rsqrt(var + LN_EPS) * lg + lb
    return yn * jax.nn.sigmoid(yn)


def _conv_prompt_body(cur_ref, prev_ref, cw_ref, cb_ref, lg_ref, lb_ref, o_ref, win_ref, *, ch):
    j = pl.program_id(1)
    win_ref[0:ch, :] = jnp.where(j == 0, 0.0, prev_ref[...])
    win_ref[ch:2 * ch, :] = cur_ref[...]
    acc = jnp.zeros((ch, CONV_CH), F32)
    for w in range(CONV_WIDTH):
        acc = acc + win_ref[ch - CONV_HIST + w:2 * ch - CONV_HIST + w, :] * cw_ref[w:w + 1, :]
    o_ref[...] = _ln_swish(acc + cb_ref[...], lg_ref[...], lb_ref[...]).astype(o_ref.dtype)


def _conv_prompt(g, cw, cb, lg, lb, batch, seq, ch):
    nj = seq // ch
    par = pl.BlockSpec((1, CONV_CH), lambda b, j: (0, 0))
    return pl.pallas_call(
        functools.partial(_conv_prompt_body, ch=ch),
        grid=(batch, nj),
        in_specs=[pl.BlockSpec((ch, CONV_CH), lambda b, j: (b * nj + j, 0)),
                  pl.BlockSpec((ch, CONV_CH), lambda b, j: (b * nj + jnp.maximum(j - 1, 0), 0)),
                  pl.BlockSpec((CONV_WIDTH, CONV_CH), lambda b, j: (0, 0)), par, par, par],
        out_specs=pl.BlockSpec((ch, CONV_CH), lambda b, j: (b * nj + j, 0)),
        out_shape=jax.ShapeDtypeStruct((batch * seq, CONV_CH), BF16),
        scratch_shapes=[pltpu.VMEM((2 * ch, CONV_CH), F32)],
        compiler_params=_cparams(("parallel", "parallel")),
        name="conv_prompt",
    )(g, g, cw, cb, lg, lb)


def _conv_sample_body(gh_ref, cw_ref, cb_ref, lg_ref, lb_ref, o_ref, *, t_new):
    for i in range(t_new):
        acc = jnp.zeros(gh_ref.shape[1:], F32)
        for w in range(CONV_WIDTH):
            acc = acc + gh_ref[i + w] * cw_ref[w:w + 1, :]
        o_ref[i] = _ln_swish(acc + cb_ref[...], lg_ref[...], lb_ref[...]).astype(o_ref.dtype)


def _conv_sample(gh_t, cw, cb, lg, lb):
    lh, n_seq, _ = gh_t.shape
    t_new = lh - CONV_HIST
    return pl.pallas_call(
        functools.partial(_conv_sample_body, t_new=t_new),
        out_shape=jax.ShapeDtypeStruct((t_new, n_seq, CONV_CH), BF16),
        name="conv_sample",
    )(gh_t, cw, cb, lg, lb)


def _outproj_body(attn_ref, c_ref, x_ref, wo_ref, g2_ref, wq_ref, keys_ref, x1_ref, h2t_ref, st_ref):
    a = jnp.dot(attn_ref[...], wo_ref[0:ATTN_WIDTH, :], preferred_element_type=F32)
    a = a + jnp.dot(c_ref[...], wo_ref[ATTN_WIDTH:D_MODEL, :], preferred_element_type=F32)
    x1 = x_ref[...] + a
    x1_ref[...] = x1
    h2 = _rms(x1, g2_ref[...])
    h2t_ref[...] = h2.T.astype(BF16)
    qv = jnp.dot(h2.astype(BF16), wq_ref[...], preferred_element_type=F32).astype(BF16)
    nt = (((1,), (1,)), ((), ()))
    for h in range(PEER_HEADS):
        for side in range(2):
            qs = qv[:, h * D_QUERY + side * HALF_Q:h * D_QUERY + (side + 1) * HALF_Q]
            st_ref[2 * h + side] = lax.dot_general(keys_ref[side], qs, nt, preferred_element_type=F32)


def _out_projection(attn, c, x, wo_bf16, g2, wq_bf16, keys_bf16, tm):
    T = x.shape[0]
    tok = lambda i: (i, 0)
    full2 = lambda i: (0, 0)
    return pl.pallas_call(
        _outproj_body,
        grid=(T // tm,),
        in_specs=[pl.BlockSpec((tm, ATTN_WIDTH), tok), pl.BlockSpec((tm, CONV_CH), tok),
                  pl.BlockSpec((tm, D_MODEL), tok), pl.BlockSpec((D_MODEL, D_MODEL), full2),
                  pl.BlockSpec((1, D_MODEL), full2), pl.BlockSpec((D_MODEL, PEER_HEADS * D_QUERY), full2),
                  pl.BlockSpec((2, N_KEYS, HALF_Q), lambda i: (0, 0, 0))],
        out_specs=[pl.BlockSpec((tm, D_MODEL), tok), pl.BlockSpec((D_MODEL, tm), lambda i: (0, i)),
                   pl.BlockSpec((2 * PEER_HEADS, N_KEYS, tm), lambda i: (0, 0, i))],
        out_shape=[jax.ShapeDtypeStruct((T, D_MODEL), F32), jax.ShapeDtypeStruct((D_MODEL, T), BF16),
                   jax.ShapeDtypeStruct((2 * PEER_HEADS, N_KEYS, T), F32)],
        compiler_params=_cparams(("parallel",)),
        name="out_projection",
    )(attn, c, x, wo_bf16, g2, wq_bf16, keys_bf16)


def _top16(s, order):
    rank = jnp.full(s.shape, NOT_SELECTED, F32)
    slot = lax.broadcasted_iota(jnp.int32, (PEER_TOPK, s.shape[1]), 0)
    vals = jnp.zeros((PEER_TOPK, s.shape[1]), F32)
    for a in range(PEER_TOPK):
        mx = jnp.max(s, axis=0, keepdims=True)
        first = jnp.min(jnp.where(s == mx, order, 1e9), axis=0, keepdims=True)
        sel = order == first
        rank = jnp.where(sel, float(a), rank)
        s = jnp.where(sel, -jnp.inf, s)
        vals = jnp.where(slot == a, mx, vals)
    return rank, vals


def _cex(a, b):
    if a is None or b is None:
        return (b if a is None else a), None
    return jnp.maximum(a, b), jnp.minimum(a, b)


def _bitonic_sort_desc(xs):
    xs, n, k = list(xs), len(xs), 2
    while k <= n:
        j = k // 2
        while j >= 1:
            for i in range(n):
                if i ^ j > i:
                    hi, lo = _cex(xs[i], xs[i ^ j])
                    xs[i], xs[i ^ j] = (hi, lo) if (i & k) == 0 else (lo, hi)
            j //= 2
        k *= 2
    return xs


def _bitonic_merge_desc(xs):
    xs, j = list(xs), len(xs) // 2
    while j >= 1:
        for i in range(len(xs)):
            if (i & j) == 0:
                xs[i], xs[i + j] = _cex(xs[i], xs[i + j])
        j //= 2
    return xs


def _top16_values(slabs):
    K = PEER_TOPK
    xs = _bitonic_sort_desc(list(slabs) + [None] * (K - len(slabs)))
    for shift in (4, 2, 1):
        other = [None if x is None else pltpu.roll(x, shift, 0) for x in xs]
        xs = _bitonic_merge_desc([_cex(xs[i], other[K - 1 - i])[0] for i in range(K)])
    return xs


def _search16(test, v):
    sel = jnp.where
    c1 = test(v[7])
    c2 = test(sel(c1, v[11], v[3]))
    c3 = test(sel(c1, sel(c2, v[13], v[9]), sel(c2, v[5], v[1])))
    c4 = test(sel(c1, sel(c2, sel(c3, v[14], v[12]), sel(c3, v[10], v[8])),
                  sel(c2, sel(c3, v[6], v[4]), sel(c3, v[2], v[0]))))
    return sel(c1, 8.0, 0.0) + sel(c2, 4.0, 0.0) + sel(c3, 2.0, 0.0) + sel(c4, 1.0, 0.0)


def _rows_sum(x):
    for shift in (4, 2, 1):
        x = x + pltpu.roll(x, shift, 0)
    return x


def _select_fast(s1, s2):
    K, R = PEER_TOPK, 8
    x1 = [s1[R * i:R * (i + 1)] for i in range(N_KEYS // R)]
    x2 = [s2[R * i:R * (i + 1)] for i in range(N_KEYS // R)]
    v1, v2 = _top16_values(x1), _top16_values(x2)
    row = lax.broadcasted_iota(jnp.int32, x1[0].shape, 0)

    def rows_of(vals):
        out = vals[0]
        for b in range(1, R):
            out = jnp.where(row == b, vals[b], out)
        return out

    v2_lo, v2_hi, v1_hi = rows_of(v2[0:R]), rows_of(v2[R:K]), rows_of(v1[R:K])
    cand = [v1[0] + v2_lo, v1[0] + v2_hi] + [v1[a] + v2_lo for a in range(1, R)] + [v1_hi + v2[0]]
    cs = _top16_values(cand)
    tau = cs[K - 1]
    z = jnp.exp(cs[0] - cs[0])
    for a in range(1, K):
        z = z + jnp.exp(cs[a] - cs[0])

    def count_ge(xs, t):
        n = jnp.zeros(xs[0].shape, F32)
        for x in xs:
            n = n + jnp.where(x >= t, 1.0, 0.0)
        return _rows_sum(n)

    tie = (count_ge(x1, v1[K - 1]) != K) | (count_ge(x2, v2[K - 1]) != K) | (count_ge(cand, tau) != K)
    for a in range(K - 1):
        tie = tie | (v1[a] == v1[a + 1]) | (v2[a] == v2[a + 1])

    p1, width, p2, r2 = [], [], [], []
    for x in x1:
        in1 = x >= v1[K - 1]
        wid = _search16(lambda vb, x=x: (x + vb) >= tau, v2) + jnp.where((x + v2[K - 1]) >= tau, 1.0, 0.0)
        p1.append(jnp.where(in1, jnp.exp(x - v1[0]) / z, 0.0))
        width.append(jnp.where(in1, wid, 0.0))
    for x in x2:
        in2 = x >= v2[K - 1]
        p2.append(jnp.where(in2, jnp.exp(x - v2[0]), 0.0))
        r2.append(jnp.where(in2, _search16(lambda vb, x=x: vb > x, v2), NOT_SELECTED))
    cat = lambda parts: jnp.concatenate(parts, axis=0)
    return cat(p1), cat(width), cat(p2), cat(r2), tie


def _select_exact(s1, s2, key_order, cand_order):
    K = PEER_TOPK
    half = K // 2
    slot16 = lax.broadcasted_iota(jnp.int32, (K, s1.shape[1]), 0)
    rank1, v1 = _top16(s1, key_order)
    rank2, v2 = _top16(s2, key_order)
    cand = [v1[0:1, :] + v2]
    cand += [v1[a:a + 1, :] + v2[0:half, :] for a in range(1, half)]
    cand += [v1[half:K, :] + v2[0:1, :]]
    cand = jnp.concatenate(cand, axis=0)
    crank, _ = _top16(cand, cand_order)
    chosen = crank < float(K)
    top = v1[0:1, :] + v2[0:1, :]
    z = jnp.sum(jnp.where(chosen, jnp.exp(cand - top), 0.0), axis=0, keepdims=True)
    chosen_f = chosen.astype(F32)
    count = jnp.zeros((K, s1.shape[1]), F32)
    count = jnp.where(slot16 == 0, jnp.sum(chosen_f[0:K], axis=0, keepdims=True), count)
    for a in range(1, half):
        lo = K + (a - 1) * half
        count = jnp.where(slot16 == a, jnp.sum(chosen_f[lo:lo + half], axis=0, keepdims=True), count)
    count = jnp.concatenate([count[0:half], chosen_f[K + (half - 1) * half:]], axis=0)
    width = jnp.zeros(s1.shape, F32)
    for a in range(K):
        width = jnp.where(rank1 == float(a), count[a:a + 1, :], width)
    in1 = rank1 < float(K)
    in2 = rank2 < float(K)
    return (jnp.where(in1, jnp.exp(s1 - v1[0:1, :]) / z, 0.0), width,
            jnp.where(in2, jnp.exp(s2 - v2[0:1, :]), 0.0), rank2)


def _peer_select_body(st_ref, p1_ref, c1_ref, p2_ref, r2_ref, *, tl):
    K = PEER_TOPK
    half = K // 2
    key_order = lax.broadcasted_iota(jnp.int32, (N_KEYS, LANES), 0).astype(F32)
    sub = lax.broadcasted_iota(jnp.int32, (half, LANES), 0)
    flat = [lax.broadcasted_iota(jnp.int32, (K, LANES), 0)]
    flat += [a * K + sub for a in range(1, half)]
    flat += [(half + sub) * K]
    cand_order = jnp.concatenate(flat, axis=0).astype(F32)

    def one(idx, carry):
        h = idx // (tl // LANES)
        ls = pl.multiple_of((idx % (tl // LANES)) * LANES, LANES)
        lanes = pl.ds(ls, LANES)
        s1 = st_ref[2 * h, :, lanes]
        s2 = st_ref[2 * h + 1, :, lanes]

        def write(p1, width, p2, r2):
            p1_ref[h, :, lanes] = p1
            c1_ref[h, :, lanes] = width
            p2_ref[h, :, lanes] = p2.astype(p2_ref.dtype)
            r2_ref[h, :, lanes] = r2.astype(r2_ref.dtype)

        *fast, tie = _select_fast(s1, s2)
        write(*fast)

        @pl.when(jnp.max(jnp.where(tie, 1.0, 0.0)) > 0.0)
        def _():
            write(*_select_exact(s1, s2, key_order, cand_order))

        return carry

    lax.fori_loop(0, PEER_HEADS * (tl // LANES), one, 0)


def _peer_select(st, tl):
    T = st.shape[2]
    out_blk = pl.BlockSpec((PEER_HEADS, N_KEYS, tl), lambda i: (0, 0, i))
    shp = jax.ShapeDtypeStruct((PEER_HEADS, N_KEYS, T), F32)
    return pl.pallas_call(
        functools.partial(_peer_select_body, tl=tl),
        grid=(T // tl,),
        in_specs=[pl.BlockSpec((2 * PEER_HEADS, N_KEYS, tl), lambda i: (0, 0, i))],
        out_specs=[out_blk] * 4,
        out_shape=[shp, shp, jax.ShapeDtypeStruct(shp.shape, BF16), jax.ShapeDtypeStruct(shp.shape, BF16)],
        compiler_params=_cparams(("parallel",)),
        name="peer_select",
    )(st)


def _peer_body(h2t_ref, p1_ref, c1_ref, p2_in_ref, r2_in_ref, u_ref, vt_ref, o_ref, a_ref, wg_ref, p2_ref, r2_ref,
               *, te):
    e = pl.program_id(1)

    @pl.when(e == 0)
    def _():
        o_ref[...] = jnp.zeros(o_ref.shape, F32)
        p2_ref[...] = p2_in_ref[...]
        r2_ref[...] = r2_in_ref[...]

    tm = o_ref.shape[1]
    a_ref[...] = jnp.dot(u_ref[...], h2t_ref[...], preferred_element_type=F32)
    for j in range(te // N_KEYS):
        rows = slice(j * N_KEYS, (j + 1) * N_KEYS)
        for lt in range(tm // LANES):
            lanes = slice(lt * LANES, (lt + 1) * LANES)
            tile = (N_KEYS, LANES)
            w = jnp.zeros(tile, BF16)
            for h in range(PEER_HEADS):
                width = jnp.broadcast_to(c1_ref[h, j:j + 1, lanes], tile).astype(BF16)
                pfirst = jnp.broadcast_to(p1_ref[h, j:j + 1, lanes], tile).astype(BF16)
                keep = r2_ref[h, :, lanes] < width
                w = w + jnp.where(keep, p2_ref[h, :, lanes], jnp.zeros(tile, BF16)) * pfirst
            act = jax.nn.gelu(a_ref[rows, lanes], approximate=True).astype(BF16)
            wg_ref[rows, lanes] = w * act
    o_ref[...] += jnp.dot(vt_ref[...], wg_ref[...], preferred_element_type=F32)


def _peer_experts(h2t, p1, c1, p2, r2, u_bf16, vt_bf16, tm, te):
    T = h2t.shape[1]
    firsts = te // N_KEYS
    sel_all = pl.BlockSpec((PEER_HEADS, N_KEYS, tm), lambda i, e: (0, 0, i))
    sel_blk = pl.BlockSpec((PEER_HEADS, firsts, tm), lambda i, e: (0, e, i))
    return pl.pallas_call(
        functools.partial(_peer_body, te=te),
        grid=(T // tm, N_EXPERTS // te),
        in_specs=[pl.BlockSpec((D_MODEL, tm), lambda i, e: (0, i)), sel_blk, sel_blk, sel_all, sel_all,
                  pl.BlockSpec((te, D_MODEL), lambda i, e: (e, 0)),
                  pl.BlockSpec((D_MODEL, te), lambda i, e: (0, e))],
        out_specs=pl.BlockSpec((D_MODEL, tm), lambda i, e: (0, i)),
        out_shape=jax.ShapeDtypeStruct((D_MODEL, T), F32),
        scratch_shapes=[pltpu.VMEM((te, tm), F32), pltpu.VMEM((te, tm), BF16),
                        pltpu.VMEM((PEER_HEADS, N_KEYS, tm), BF16), pltpu.VMEM((PEER_HEADS, N_KEYS, tm), BF16)],
        compiler_params=_cparams(("parallel", "arbitrary")),
        name="peer_experts",
    )(h2t, p1, c1, p2, r2, u_bf16, vt_bf16)


def _final_body(x1_ref, ot_ref, g_ref, y_ref):
    y_ref[...] = _rms(x1_ref[...] + ot_ref[...].T, g_ref[...])


def _final_norm(x1, out_t, g, tm):
    T = x1.shape[0]
    return pl.pallas_call(
        _final_body,
        grid=(T // tm,),
        in_specs=[pl.BlockSpec((tm, D_MODEL), lambda i: (i, 0)), pl.BlockSpec((D_MODEL, tm), lambda i: (0, i)),
                  pl.BlockSpec((1, D_MODEL), lambda i: (0, 0))],
        out_specs=pl.BlockSpec((tm, D_MODEL), lambda i: (i, 0)),
        out_shape=jax.ShapeDtypeStruct((T, D_MODEL), F32),
        compiler_params=_cparams(("parallel",)),
        name="final_norm",
    )(x1, out_t, g)


TOKEN_TILE = 512
PEER_TOKEN_TILE = 1024
PEER_EXPERT_TILE = 1024
CONV_ROWS = 512


def _channel_and_final(attn, c, x, p):
    x1, h2t, st = _out_projection(attn, c, x, p["wo"], p["g2"], p["wq"], p["keys"], TOKEN_TILE)
    p1, c1, p2, r2 = _peer_select(st, TOKEN_TILE)
    out_t = _peer_experts(h2t, p1, c1, p2, r2, p["u"], p["vt"], min(PEER_TOKEN_TILE, x.shape[0]), PEER_EXPERT_TILE)
    return _final_norm(x1, out_t, p["gf"], TOKEN_TILE)


def kernel(x_prompt, x_sample, cache_k, cache_v, state_conv, norm1_g, w_in, conv_w, conv_b, conv_ln_g,
           conv_ln_b, w_out, norm2_g, w_query, sub_keys1, sub_keys2, expert_u, expert_v, final_g):
    bp, s, _ = x_prompt.shape
    bs, t, _ = x_sample.shape
    depth, _, w_buf = cache_k.shape[:3]
    assert depth == 1 and all(s % (d * BAND_BLOCK) == 0 and s // d >= 2 * BAND_BLOCK for _, d in DILATIONS)
    keep = min(DILATIONS[-1][0], s)
    l = 0
    row = lambda a: a.reshape(1, -1)
    p = dict(
        wo=w_out[l].astype(BF16), g2=row(norm2_g[l]), wq=w_query[l].astype(BF16),
        keys=jnp.stack([sub_keys1[l], sub_keys2[l]]).astype(BF16),
        u=expert_u[l].astype(BF16), vt=expert_v[l].T.astype(BF16), gf=row(final_g),
    )
    w_in_b = w_in[l].astype(BF16)
    g1 = row(norm1_g[l])
    cw, cb, lg, lb = conv_w[l], row(conv_b[l]), row(conv_ln_g[l]), row(conv_ln_b[l])

    xp = x_prompt.reshape(bp * s, D_MODEL)
    cos, sin = _rope_tables(jnp.tile(jnp.arange(s), bp))
    q, k, v, g = _in_projection(xp, g1, w_in_b, cos, sin, TOKEN_TILE)
    attn = _attention_prompt(q, k, v, bp, s)
    c = _conv_prompt(g, cw, cb, lg, lb, bp, s, CONV_ROWS)
    y_prompt = _channel_and_final(attn, c, xp, p).reshape(bp, s, D_MODEL)
    new_k_prompt = k.reshape(bp, s, N_HEADS, HEAD_DIM)[None, :, s - keep:]
    new_v_prompt = v.reshape(bp, s, N_HEADS, HEAD_DIM)[None, :, s - keep:]
    new_conv_prompt = g.reshape(bp, s, CONV_CH)[None, :, s - CONV_HIST:]

    xs = x_sample.reshape(bs * t, D_MODEL)
    assert w_buf == min(MAX_WINDOW, PAST_LEN)
    cos, sin = _rope_tables(jnp.tile(PAST_LEN + jnp.arange(t), bs))
    q, k, v, g = _in_projection(xs, g1, w_in_b, cos, sin, TOKEN_TILE)
    heads = lambda a: a.reshape(bs, t, N_HEADS, HEAD_DIM)
    attn = _attention_sample(heads(q), heads(k), heads(v), cache_k[l], cache_v[l])
    attn = attn.reshape(bs * t, ATTN_WIDTH).astype(BF16)
    g_hist = jnp.concatenate([state_conv[l], g.reshape(bs, t, CONV_CH)], axis=1)
    c = _conv_sample(jnp.transpose(g_hist, (1, 0, 2)), cw, cb, lg, lb)
    c = jnp.transpose(c, (1, 0, 2)).reshape(bs * t, CONV_CH)
    y_sample = _channel_and_final(attn, c, xs, p).reshape(bs, t, D_MODEL)
    new_k_sample = k.reshape(bs, t, N_HEADS, HEAD_DIM)[None]
    new_v_sample = v.reshape(bs, t, N_HEADS, HEAD_DIM)[None]
    new_conv_sample = g_hist[None, :, -CONV_HIST:]

    return (y_prompt, y_sample, new_k_prompt, new_v_prompt, new_conv_prompt,
            new_k_sample, new_v_sample, new_conv_sample)
```

```python
import functools

import jax
import jax.numpy as jnp
from jax import lax
from jax.experimental import pallas as pl
from jax.experimental.pallas import tpu as pltpu

F32 = jnp.float32
BF16 = jnp.bfloat16

D_MODEL = 1024
HEAD_DIM = 64
HALF_HEAD = HEAD_DIM // 2
N_HEADS = 12
ATTN_WIDTH = N_HEADS * HEAD_DIM
CONV_CH = D_MODEL - ATTN_WIDTH
CONV_WIDTH = 31
CONV_HIST = CONV_WIDTH - 1
IN_COLS = 3 * ATTN_WIDTH + 2 * CONV_CH
DILATIONS = ((128, 1), (512, 4), (2048, 16))
MAX_WINDOW = 2048
PAST_LEN = 2048
BAND_BLOCK = 128
ROPE_THETA = 10000.0
ATTN_SCALE = HEAD_DIM ** -0.5
NEG_INF = -1e30
N_KEYS = 128
N_EXPERTS = N_KEYS * N_KEYS
PEER_HEADS = 8
PEER_TOPK = 16
D_QUERY = 256
HALF_Q = D_QUERY // 2
RMS_EPS = 1e-6
LN_EPS = 1e-5

LANES = 128
HEADS_PER_LANE_TILE = LANES // HEAD_DIM
N_HEAD_TILES = ATTN_WIDTH // LANES
NOT_SELECTED = 99.0
VMEM_LIMIT = 56 * 1024 * 1024


def _cparams(semantics, flags=None):
    return pltpu.CompilerParams(dimension_semantics=semantics, vmem_limit_bytes=VMEM_LIMIT, flags=flags)


def _rms(x, g):
    return (x * lax.rsqrt(jnp.mean(x * x, axis=-1, keepdims=True) + RMS_EPS)) * g


def _inproj_body(x_ref, g_ref, w_ref, cos_ref, sin_ref, q_ref, k_ref, v_ref, gl_ref):
    hb = _rms(x_ref[...], g_ref[...]).astype(BF16)
    cos = cos_ref[...]
    sin = sin_ref[...]
    lane = lax.broadcasted_iota(jnp.int32, cos.shape, 1)
    first_half = jnp.bitwise_and(lane, HEAD_DIM - 1) < HALF_HEAD

    def rope(t):
        partner = jnp.where(first_half, pltpu.roll(t, LANES - HALF_HEAD, 1), pltpu.roll(t, HALF_HEAD, 1))
        return t * cos + partner * sin

    q = jnp.dot(hb, w_ref[:, 0:ATTN_WIDTH], preferred_element_type=F32)
    k = jnp.dot(hb, w_ref[:, ATTN_WIDTH:2 * ATTN_WIDTH], preferred_element_type=F32)
    for c in range(N_HEAD_TILES):
        sl = slice(c * LANES, (c + 1) * LANES)
        q_ref[:, sl] = rope(q[:, sl]) * ATTN_SCALE
        k_ref[:, sl] = rope(k[:, sl])
    v_ref[...] = jnp.dot(hb, w_ref[:, 2 * ATTN_WIDTH:3 * ATTN_WIDTH], preferred_element_type=F32)
    u = jnp.dot(hb, w_ref[:, 3 * ATTN_WIDTH:IN_COLS], preferred_element_type=F32)
    gl_ref[...] = u[:, :CONV_CH] * jax.nn.sigmoid(u[:, CONV_CH:])


def _in_projection(x, g, w_bf16, cos, sin, tm):
    T = x.shape[0]
    tok = lambda i: (i, 0)
    full = lambda i: (0, 0)
    return pl.pallas_call(
        _inproj_body,
        grid=(T // tm,),
        in_specs=[pl.BlockSpec((tm, D_MODEL), tok), pl.BlockSpec((1, D_MODEL), full),
                  pl.BlockSpec((D_MODEL, IN_COLS), full),
                  pl.BlockSpec((tm, LANES), tok), pl.BlockSpec((tm, LANES), tok)],
        out_specs=[pl.BlockSpec((tm, ATTN_WIDTH), tok)] * 3 + [pl.BlockSpec((tm, CONV_CH), tok)],
        out_shape=[jax.ShapeDtypeStruct((T, ATTN_WIDTH), F32)] * 3 + [jax.ShapeDtypeStruct((T, CONV_CH), F32)],
        compiler_params=_cparams(("parallel",)),
        name="in_projection",
    )(x, g, w_bf16, cos, sin)


def _rope_tables(pos):
    inv = jnp.power(ROPE_THETA, -jnp.arange(HALF_HEAD, dtype=F32) * (2.0 / HEAD_DIM))
    ang = pos.astype(F32)[:, None] * inv[None, :]
    cos, sin = jnp.cos(ang), jnp.sin(ang)
    return (jnp.concatenate([cos, cos] * HEADS_PER_LANE_TILE, axis=-1),
            jnp.concatenate([-sin, sin] * HEADS_PER_LANE_TILE, axis=-1))


ATTN_MERGE_ROWS = 256


def _attn_prompt_body(q_ref, k_ref, v_ref, o_ref, ob_ref, mb_ref, lb_ref, *, seq):
    Q = BAND_BLOCK
    lane = lax.broadcasted_iota(jnp.int32, (Q, LANES), 1)
    head_lanes = [lane < HEAD_DIM, lane >= HEAD_DIM]
    qi = lax.broadcasted_iota(jnp.int32, (Q, 2 * Q), 0)
    ki = lax.broadcasted_iota(jnp.int32, (Q, 2 * Q), 1)

    for b, (win, dil) in enumerate(DILATIONS):
        reach = win // dil
        n_blocks = (seq // dil) // Q

        def rows(start, size, dil=dil):
            return pl.ds(start, size) if dil == 1 else pl.ds(start, size, stride=dil)

        def step(idx, carry, b=b, dil=dil, reach=reach, n_blocks=n_blocks, rows=rows):
            r = idx // n_blocks
            a0 = (idx % n_blocks) * Q
            ks = jnp.maximum(a0 - Q, 0)
            q_rows = rows(r + a0 * dil, Q)
            k_rows = rows(r + ks * dil, 2 * Q)
            q = q_ref[q_rows, :]
            k = k_ref[k_rows, :].astype(BF16)
            v = v_ref[k_rows, :].astype(BF16)
            dist = (a0 - ks) + qi - ki
            valid = (dist >= 0) & (dist <= reach)
            o_full = jnp.zeros((Q, LANES), F32)
            m_full = jnp.zeros((Q, LANES), F32)
            l_full = jnp.zeros((Q, LANES), F32)
            for h in range(HEADS_PER_LANE_TILE):
                qh = jnp.where(head_lanes[h], q, 0.0).astype(BF16)
                s = lax.dot_general(qh, k, (((1,), (1,)), ((), ())), preferred_element_type=F32)
                s = jnp.where(valid, s, NEG_INF)
                m = jnp.max(s, axis=1, keepdims=True)
                p = jnp.exp(s - m)
                l = jnp.sum(p, axis=1, keepdims=True)
                pv = jnp.dot(p.astype(BF16), v, preferred_element_type=F32)
                o_full = jnp.where(head_lanes[h], pv, o_full)
                m_full = jnp.where(head_lanes[h], m, m_full)
                l_full = jnp.where(head_lanes[h], l, l_full)
            ob_ref[b, q_rows, :] = o_full
            mb_ref[b, q_rows, :] = m_full
            lb_ref[b, q_rows, :] = l_full
            return carry

        lax.fori_loop(0, dil * n_blocks, step, 0, unroll=8)

    def merge(i, carry):
        rws = pl.ds(pl.multiple_of(i * ATTN_MERGE_ROWS, ATTN_MERGE_ROWS), ATTN_MERGE_ROWS)
        n_br = len(DILATIONS)
        m_all = mb_ref[0, rws, :]
        for b in range(1, n_br):
            m_all = jnp.maximum(m_all, mb_ref[b, rws, :])
        num = jnp.zeros((ATTN_MERGE_ROWS, LANES), F32)
        den = jnp.zeros((ATTN_MERGE_ROWS, LANES), F32)
        for b in range(n_br):
            w = jnp.exp(mb_ref[b, rws, :] - m_all)
            num = num + w * ob_ref[b, rws, :]
            den = den + w * lb_ref[b, rws, :]
        o_ref[rws, :] = (num / den).astype(o_ref.dtype)
        return carry

    lax.fori_loop(0, seq // ATTN_MERGE_ROWS, merge, 0)


def _attention_prompt(q, k, v, batch, seq):
    T = q.shape[0]
    blk = pl.BlockSpec((seq, LANES), lambda b, hp: (b, hp))
    return pl.pallas_call(
        functools.partial(_attn_prompt_body, seq=seq),
        grid=(batch, N_HEAD_TILES),
        in_specs=[blk, blk, blk],
        out_specs=blk,
        out_shape=jax.ShapeDtypeStruct((T, ATTN_WIDTH), BF16),
        scratch_shapes=[pltpu.VMEM((len(DILATIONS), seq, LANES), F32)] * 3,
        compiler_params=_cparams(("parallel", "parallel")),
        name="attention_prompt",
    )(q, k, v)


SAMPLE_CHUNK = 128


def _attn_sample_body(q_ref, kn_ref, vn_ref, kt_ref, vt_ref, kg_ref, vg_ref, o_ref, s_ref, p_ref, m_ref, l_ref,
                      acc_ref, *, t_new, tail, past):
    c = pl.program_id(1)
    n_seq = q_ref.shape[-1]
    (w1, d1), (w4, d4), (w16, d16) = DILATIONS
    q = [q_ref[i] for i in range(t_new)]

    @pl.when(c == 0)
    def _():
        for i in range(t_new):
            s = [jnp.sum(q[i] * kn_ref[j], axis=0, keepdims=True) for j in range(i + 1)]
            m = s[0]
            for j in range(1, i + 1):
                m = jnp.maximum(m, s[j])
            l = jnp.zeros((1, n_seq), F32)
            acc = jnp.zeros((HEAD_DIM, n_seq), F32)
            for j in range(i + 1):
                p = (float(len(DILATIONS)) if j == i else 1.0) * jnp.exp(s[j] - m)
                l = l + p
                acc = acc + p * vn_ref[j]
            m_ref[i] = m
            l_ref[i] = l
            acc_ref[i] = acc

    def chunk(k_row_of, v_row_of):
        def score_row(r, carry):
            k_row = k_row_of(r)
            for i in range(t_new):
                s_ref[i, pl.ds(r, 1), :] = jnp.sum(q[i] * k_row, axis=0, keepdims=True)
            return carry

        lax.fori_loop(0, SAMPLE_CHUNK, score_row, 0, unroll=2)

        slot = c * SAMPLE_CHUNK + lax.broadcasted_iota(jnp.int32, (SAMPLE_CHUNK, n_seq), 0)
        in_tail = slot < tail
        cache_row = slot + (past - tail)
        grp = jnp.right_shift(slot - tail, shift)
        grp_row = jnp.bitwise_and(slot - tail, low)
        for i in range(t_new):
            in_d1 = in_tail & (cache_row >= past + i - w1 // d1)
            in_d4 = in_tail & (jnp.bitwise_and(cache_row - i, d4 - 1) == 0) & (cache_row >= past + i - w4)
            in_d16 = (~in_tail) & (grp_row == i) & (grp >= past // d16 - w16 // d16)
            mult = in_d1.astype(F32) + in_d4.astype(F32) + in_d16.astype(F32)
            s = jnp.where(mult > 0, s_ref[i], NEG_INF)
            m_old = m_ref[i]
            m_new = jnp.maximum(m_old, jnp.max(s, axis=0, keepdims=True))
            alpha = jnp.exp(m_old - m_new)
            p = mult * jnp.exp(s - m_new)
            p_ref[i] = p
            l_ref[i] = alpha * l_ref[i] + jnp.sum(p, axis=0, keepdims=True)
            m_ref[i] = m_new
            acc_ref[i] = acc_ref[i] * alpha

        def value_row(r, acc):
            v_row = v_row_of(r)
            return tuple(acc[i] + p_ref[i, pl.ds(r, 1), :] * v_row for i in range(t_new))

        acc = lax.fori_loop(0, SAMPLE_CHUNK, value_row, tuple(acc_ref[i] for i in range(t_new)), unroll=2)
        for i in range(t_new):
            acc_ref[i] = acc[i]

    shift, low = t_new.bit_length() - 1, t_new - 1
    n_groups = SAMPLE_CHUNK // t_new

    def strided_chunk(k_row_of, v_row_of, in_tail):
        def score_group(g, carry):
            for i in range(t_new):
                s_ref[i, pl.ds(g, 1), :] = jnp.sum(q[i] * k_row_of(g, i), axis=0, keepdims=True)
            return carry

        lax.fori_loop(0, n_groups, score_group, 0, unroll=2)

        grp_in_chunk = lax.broadcasted_iota(jnp.int32, (n_groups, n_seq), 0)
        for i in range(t_new):
            slot = c * SAMPLE_CHUNK + grp_in_chunk * t_new + i
            if in_tail:
                member = slot + (past - tail) >= past + i - w4
            else:
                member = jnp.right_shift(slot - tail, shift) >= past // d16 - w16 // d16
            s = jnp.where(member, s_ref[i, 0:n_groups, :], NEG_INF)
            m_old = m_ref[i]
            m_new = jnp.maximum(m_old, jnp.max(s, axis=0, keepdims=True))
            alpha = jnp.exp(m_old - m_new)
            p = jnp.where(member, jnp.exp(s - m_new), 0.0)
            p_ref[i, 0:n_groups, :] = p
            l_ref[i] = alpha * l_ref[i] + jnp.sum(p, axis=0, keepdims=True)
            m_ref[i] = m_new
            acc_ref[i] = acc_ref[i] * alpha

        def value_group(g, acc):
            return tuple(acc[i] + p_ref[i, pl.ds(g, 1), :] * v_row_of(g, i) for i in range(t_new))

        acc = lax.fori_loop(0, n_groups, value_group, tuple(acc_ref[i] for i in range(t_new)), unroll=2)
        for i in range(t_new):
            acc_ref[i] = acc[i]

    tail_chunks = tail // SAMPLE_CHUNK
    dense_chunks = -(-(w1 // d1) // SAMPLE_CHUNK)

    @pl.when(c < tail_chunks - dense_chunks)
    def _():
        strided_chunk(lambda g, i: kt_ref[g * t_new + i], lambda g, i: vt_ref[g * t_new + i], True)

    @pl.when((c >= tail_chunks - dense_chunks) & (c < tail_chunks))
    def _():
        chunk(lambda r: kt_ref[r], lambda r: vt_ref[r])

    @pl.when(c >= tail_chunks)
    def _():
        strided_chunk(lambda g, i: kg_ref[g, i], lambda g, i: vg_ref[g, i], False)

    @pl.when(c == pl.num_programs(1) - 1)
    def _():
        for i in range(t_new):
            o_ref[i] = acc_ref[i] / l_ref[i]


def _attention_sample(q, k_new, v_new, cache_k, cache_v):
    n_seq, t_new = q.shape[:2]
    past = cache_k.shape[1]
    (w1, d1), (w4, d4), (w16, d16) = DILATIONS
    tail = w4
    assert past % d16 == 0 and past >= w16 and w1 <= tail and t_new & (t_new - 1) == 0
    assert t_new == d4 and (past - tail) % d4 == 0 and d1 == 1
    tail_chunks = tail // SAMPLE_CHUNK
    group_chunks = (past // d16 * t_new) // SAMPLE_CHUNK
    groups_per_chunk = SAMPLE_CHUNK // t_new
    assert tail % SAMPLE_CHUNK == 0 and past % SAMPLE_CHUNK == 0 and (past // d16) % groups_per_chunk == 0
    lanes_last = lambda a: jnp.transpose(a, (1, 2, 3, 0))
    grouped = lambda a: a.reshape((past // d16, d16) + a.shape[1:])
    ck, cv = lanes_last(cache_k), lanes_last(cache_v)
    new_blk = pl.BlockSpec((t_new, None, HEAD_DIM, n_seq), lambda h, c: (0, h, 0, 0))
    tail_blk = pl.BlockSpec((SAMPLE_CHUNK, None, HEAD_DIM, n_seq),
                            lambda h, c: (past // SAMPLE_CHUNK - tail_chunks + jnp.minimum(c, tail_chunks - 1), h, 0, 0))
    group_blk = pl.BlockSpec((groups_per_chunk, t_new, None, HEAD_DIM, n_seq),
                             lambda h, c: (jnp.maximum(c - tail_chunks, 0), 0, h, 0, 0))
    out = pl.pallas_call(
        functools.partial(_attn_sample_body, t_new=t_new, tail=tail, past=past),
        grid=(N_HEADS, tail_chunks + group_chunks),
        in_specs=[new_blk, new_blk, new_blk, tail_blk, tail_blk, group_blk, group_blk],
        out_specs=pl.BlockSpec((None, t_new, HEAD_DIM, n_seq), lambda h, c: (h, 0, 0, 0)),
        out_shape=jax.ShapeDtypeStruct((N_HEADS, t_new, HEAD_DIM, n_seq), F32),
        scratch_shapes=[pltpu.VMEM((t_new, SAMPLE_CHUNK, n_seq), F32), pltpu.VMEM((t_new, SAMPLE_CHUNK, n_seq), F32),
                        pltpu.VMEM((t_new, 1, n_seq), F32), pltpu.VMEM((t_new, 1, n_seq), F32),
                        pltpu.VMEM((t_new, HEAD_DIM, n_seq), F32)],
        compiler_params=_cparams(("parallel", "arbitrary")),
        name="attention_sample",
    )(lanes_last(q), lanes_last(k_new), lanes_last(v_new), ck, cv, grouped(ck), grouped(cv))
    return jnp.transpose(out, (3, 1, 0, 2)).reshape(n_seq, t_new, ATTN_WIDTH)


def _ln_swish(y, lg, lb):
    mu = jnp.mean(y, axis=-1, keepdims=True)
    var = jnp.mean(jnp.square(y - mu), axis=-1, keepdims=True)
    yn = (y - mu) * lax.rsqrt(var + LN_EPS) * lg + lb
    return yn * jax.nn.sigmoid(yn)


def _conv_prompt_body(cur_ref, prev_ref, cw_ref, cb_ref, lg_ref, lb_ref, o_ref, win_ref, *, ch):
    j = pl.program_id(1)
    win_ref[0:ch, :] = jnp.where(j == 0, 0.0, prev_ref[...])
    win_ref[ch:2 * ch, :] = cur_ref[...]
    acc = jnp.zeros((ch, CONV_CH), F32)
    for w in range(CONV_WIDTH):
        acc = acc + win_ref[ch - CONV_HIST + w:2 * ch - CONV_HIST + w, :] * cw_ref[w:w + 1, :]
    o_ref[...] = _ln_swish(acc + cb_ref[...], lg_ref[...], lb_ref[...]).astype(o_ref.dtype)


def _conv_prompt(g, cw, cb, lg, lb, batch, seq, ch):
    nj = seq // ch
    par = pl.BlockSpec((1, CONV_CH), lambda b, j: (0, 0))
    return pl.pallas_call(
        functools.partial(_conv_prompt_body, ch=ch),
        grid=(batch, nj),
        in_specs=[pl.BlockSpec((ch, CONV_CH), lambda b, j: (b * nj + j, 0)),
                  pl.BlockSpec((ch, CONV_CH), lambda b, j: (b * nj + jnp.maximum(j - 1, 0), 0)),
                  pl.BlockSpec((CONV_WIDTH, CONV_CH), lambda b, j: (0, 0)), par, par, par],
        out_specs=pl.BlockSpec((ch, CONV_CH), lambda b, j: (b * nj + j, 0)),
        out_shape=jax.ShapeDtypeStruct((batch * seq, CONV_CH), BF16),
        scratch_shapes=[pltpu.VMEM((2 * ch, CONV_CH), F32)],
        compiler_params=_cparams(("parallel", "parallel")),
        name="conv_prompt",
    )(g, g, cw, cb, lg, lb)


def _conv_sample_body(gh_ref, cw_ref, cb_ref, lg_ref, lb_ref, o_ref, *, t_new):
    for i in range(t_new):
        acc = jnp.zeros(gh_ref.shape[1:], F32)
        for w in range(CONV_WIDTH):
            acc = acc + gh_ref[i + w] * cw_ref[w:w + 1, :]
        o_ref[i] = _ln_swish(acc + cb_ref[...], lg_ref[...], lb_ref[...]).astype(o_ref.dtype)


def _conv_sample(gh_t, cw, cb, lg, lb):
    lh, n_seq, _ = gh_t.shape
    t_new = lh - CONV_HIST
    return pl.pallas_call(
        functools.partial(_conv_sample_body, t_new=t_new),
        out_shape=jax.ShapeDtypeStruct((t_new, n_seq, CONV_CH), BF16),
        name="conv_sample",
    )(gh_t, cw, cb, lg, lb)


def _outproj_body(attn_ref, c_ref, x_ref, wo_ref, g2_ref, wq_ref, keys_ref, x1_ref, h2t_ref, st_ref):
    a = jnp.dot(attn_ref[...], wo_ref[0:ATTN_WIDTH, :], preferred_element_type=F32)
    a = a + jnp.dot(c_ref[...], wo_ref[ATTN_WIDTH:D_MODEL, :], preferred_element_type=F32)
    x1 = x_ref[...] + a
    x1_ref[...] = x1
    h2 = _rms(x1, g2_ref[...])
    h2t_ref[...] = h2.T.astype(BF16)
    qv = jnp.dot(h2.astype(BF16), wq_ref[...], preferred_element_type=F32).astype(BF16)
    nt = (((1,), (1,)), ((), ()))
    for h in range(PEER_HEADS):
        for side in range(2):
            qs = qv[:, h * D_QUERY + side * HALF_Q:h * D_QUERY + (side + 1) * HALF_Q]
            st_ref[2 * h + side] = lax.dot_general(keys_ref[side], qs, nt, preferred_element_type=F32)


def _out_projection(attn, c, x, wo_bf16, g2, wq_bf16, keys_bf16, tm):
    T = x.shape[0]
    tok = lambda i: (i, 0)
    full2 = lambda i: (0, 0)
    return pl.pallas_call(
        _outproj_body,
        grid=(T // tm,),
        in_specs=[pl.BlockSpec((tm, ATTN_WIDTH), tok), pl.BlockSpec((tm, CONV_CH), tok),
                  pl.BlockSpec((tm, D_MODEL), tok), pl.BlockSpec((D_MODEL, D_MODEL), full2),
                  pl.BlockSpec((1, D_MODEL), full2), pl.BlockSpec((D_MODEL, PEER_HEADS * D_QUERY), full2),
                  pl.BlockSpec((2, N_KEYS, HALF_Q), lambda i: (0, 0, 0))],
        out_specs=[pl.BlockSpec((tm, D_MODEL), tok), pl.BlockSpec((D_MODEL, tm), lambda i: (0, i)),
                   pl.BlockSpec((2 * PEER_HEADS, N_KEYS, tm), lambda i: (0, 0, i))],
        out_shape=[jax.ShapeDtypeStruct((T, D_MODEL), F32), jax.ShapeDtypeStruct((D_MODEL, T), BF16),
                   jax.ShapeDtypeStruct((2 * PEER_HEADS, N_KEYS, T), F32)],
        compiler_params=_cparams(("parallel",)),
        name="out_projection",
    )(attn, c, x, wo_bf16, g2, wq_bf16, keys_bf16)


def _top16(s, order):
    rank = jnp.full(s.shape, NOT_SELECTED, F32)
    slot = lax.broadcasted_iota(jnp.int32, (PEER_TOPK, s.shape[1]), 0)
    vals = jnp.zeros((PEER_TOPK, s.shape[1]), F32)
    for a in range(PEER_TOPK):
        mx = jnp.max(s, axis=0, keepdims=True)
        first = jnp.min(jnp.where(s == mx, order, 1e9), axis=0, keepdims=True)
        sel = order == first
        rank = jnp.where(sel, float(a), rank)
        s = jnp.where(sel, -jnp.inf, s)
        vals = jnp.where(slot == a, mx, vals)
    return rank, vals


def _cex(a, b):
    if a is None or b is None:
        return (b if a is None else a), None
    return jnp.maximum(a, b), jnp.minimum(a, b)


def _bitonic_sort_desc(xs):
    xs, n, k = list(xs), len(xs), 2
    while k <= n:
        j = k // 2
        while j >= 1:
            for i in range(n):
                if i ^ j > i:
                    hi, lo = _cex(xs[i], xs[i ^ j])
                    xs[i], xs[i ^ j] = (hi, lo) if (i & k) == 0 else (lo, hi)
            j //= 2
        k *= 2
    return xs


def _bitonic_merge_desc(xs):
    xs, j = list(xs), len(xs) // 2
    while j >= 1:
        for i in range(len(xs)):
            if (i & j) == 0:
                xs[i], xs[i + j] = _cex(xs[i], xs[i + j])
        j //= 2
    return xs


def _top16_values(slabs):
    K = PEER_TOPK
    xs = _bitonic_sort_desc(list(slabs) + [None] * (K - len(slabs)))
    for shift in (4, 2, 1):
        other = [None if x is None else pltpu.roll(x, shift, 0) for x in xs]
        xs = _bitonic_merge_desc([_cex(xs[i], other[K - 1 - i])[0] for i in range(K)])
    return xs


def _search16(test, v):
    sel = jnp.where
    c1 = test(v[7])
    c2 = test(sel(c1, v[11], v[3]))
    c3 = test(sel(c1, sel(c2, v[13], v[9]), sel(c2, v[5], v[1])))
    c4 = test(sel(c1, sel(c2, sel(c3, v[14], v[12]), sel(c3, v[10], v[8])),
                  sel(c2, sel(c3, v[6], v[4]), sel(c3, v[2], v[0]))))
    return sel(c1, 8.0, 0.0) + sel(c2, 4.0, 0.0) + sel(c3, 2.0, 0.0) + sel(c4, 1.0, 0.0)


def _rows_sum(x):
    for shift in (4, 2, 1):
        x = x + pltpu.roll(x, shift, 0)
    return x


def _select_fast(s1, s2):
    K, R = PEER_TOPK, 8
    x1 = [s1[R * i:R * (i + 1)] for i in range(N_KEYS // R)]
    x2 = [s2[R * i:R * (i + 1)] for i in range(N_KEYS // R)]
    v1, v2 = _top16_values(x1), _top16_values(x2)
    row = lax.broadcasted_iota(jnp.int32, x1[0].shape, 0)

    def rows_of(vals):
        out = vals[0]
        for b in range(1, R):
            out = jnp.where(row == b, vals[b], out)
        return out

    v2_lo, v2_hi, v1_hi = rows_of(v2[0:R]), rows_of(v2[R:K]), rows_of(v1[R:K])
    cand = [v1[0] + v2_lo, v1[0] + v2_hi] + [v1[a] + v2_lo for a in range(1, R)] + [v1_hi + v2[0]]
    cs = _top16_values(cand)
    tau = cs[K - 1]
    z = jnp.exp(cs[0] - cs[0])
    for a in range(1, K):
        z = z + jnp.exp(cs[a] - cs[0])

    def count_ge(xs, t):
        n = jnp.zeros(xs[0].shape, F32)
        for x in xs:
            n = n + jnp.where(x >= t, 1.0, 0.0)
        return _rows_sum(n)

    tie = (count_ge(x1, v1[K - 1]) != K) | (count_ge(x2, v2[K - 1]) != K) | (count_ge(cand, tau) != K)
    for a in range(K - 1):
        tie = tie | (v1[a] == v1[a + 1]) | (v2[a] == v2[a + 1])

    p1, width, p2, r2 = [], [], [], []
    for x in x1:
        in1 = x >= v1[K - 1]
        wid = _search16(lambda vb, x=x: (x + vb) >= tau, v2) + jnp.where((x + v2[K - 1]) >= tau, 1.0, 0.0)
        p1.append(jnp.where(in1, jnp.exp(x - v1[0]) / z, 0.0))
        width.append(jnp.where(in1, wid, 0.0))
    for x in x2:
        in2 = x >= v2[K - 1]
        p2.append(jnp.where(in2, jnp.exp(x - v2[0]), 0.0))
        r2.append(jnp.where(in2, _search16(lambda vb, x=x: vb > x, v2), NOT_SELECTED))
    cat = lambda parts: jnp.concatenate(parts, axis=0)
    return cat(p1), cat(width), cat(p2), cat(r2), tie


def _select_exact(s1, s2, key_order, cand_order):
    K = PEER_TOPK
    half = K // 2
    slot16 = lax.broadcasted_iota(jnp.int32, (K, s1.shape[1]), 0)
    rank1, v1 = _top16(s1, key_order)
    rank2, v2 = _top16(s2, key_order)
    cand = [v1[0:1, :] + v2]
    cand += [v1[a:a + 1, :] + v2[0:half, :] for a in range(1, half)]
    cand += [v1[half:K, :] + v2[0:1, :]]
    cand = jnp.concatenate(cand, axis=0)
    crank, _ = _top16(cand, cand_order)
    chosen = crank < float(K)
    top = v1[0:1, :] + v2[0:1, :]
    z = jnp.sum(jnp.where(chosen, jnp.exp(cand - top), 0.0), axis=0, keepdims=True)
    chosen_f = chosen.astype(F32)
    count = jnp.zeros((K, s1.shape[1]), F32)
    count = jnp.where(slot16 == 0, jnp.sum(chosen_f[0:K], axis=0, keepdims=True), count)
    for a in range(1, half):
        lo = K + (a - 1) * half
        count = jnp.where(slot16 == a, jnp.sum(chosen_f[lo:lo + half], axis=0, keepdims=True), count)
    count = jnp.concatenate([count[0:half], chosen_f[K + (half - 1) * half:]], axis=0)
    width = jnp.zeros(s1.shape, F32)
    for a in range(K):
        width = jnp.where(rank1 == float(a), count[a:a + 1, :], width)
    in1 = rank1 < float(K)
    in2 = rank2 < float(K)
    return (jnp.where(in1, jnp.exp(s1 - v1[0:1, :]) / z, 0.0), width,
            jnp.where(in2, jnp.exp(s2 - v2[0:1, :]), 0.0), rank2)


def _peer_select_body(st_ref, p1_ref, c1_ref, p2_ref, r2_ref, *, tl):
    K = PEER_TOPK
    half = K // 2
    key_order = lax.broadcasted_iota(jnp.int32, (N_KEYS, LANES), 0).astype(F32)
    sub = lax.broadcasted_iota(jnp.int32, (half, LANES), 0)
    flat = [lax.broadcasted_iota(jnp.int32, (K, LANES), 0)]
    flat += [a * K + sub for a in range(1, half)]
    flat += [(half + sub) * K]
    cand_order = jnp.concatenate(flat, axis=0).astype(F32)

    def one(idx, carry):
        h = idx // (tl // LANES)
        ls = pl.multiple_of((idx % (tl // LANES)) * LANES, LANES)
        lanes = pl.ds(ls, LANES)
        s1 = st_ref[2 * h, :, lanes]
        s2 = st_ref[2 * h + 1, :, lanes]

        def write(p1, width, p2, r2):
            p1_ref[h, :, lanes] = p1
            c1_ref[h, :, lanes] = width
            p2_ref[h, :, lanes] = p2.astype(p2_ref.dtype)
            r2_ref[h, :, lanes] = r2.astype(r2_ref.dtype)

        *fast, tie = _select_fast(s1, s2)
        write(*fast)

        @pl.when(jnp.max(jnp.where(tie, 1.0, 0.0)) > 0.0)
        def _():
            write(*_select_exact(s1, s2, key_order, cand_order))

        return carry

    lax.fori_loop(0, PEER_HEADS * (tl // LANES), one, 0)


def _peer_select(st, tl):
    T = st.shape[2]
    out_blk = pl.BlockSpec((PEER_HEADS, N_KEYS, tl), lambda i: (0, 0, i))
    shp = jax.ShapeDtypeStruct((PEER_HEADS, N_KEYS, T), F32)
    return pl.pallas_call(
        functools.partial(_peer_select_body, tl=tl),
        grid=(T // tl,),
        in_specs=[pl.BlockSpec((2 * PEER_HEADS, N_KEYS, tl), lambda i: (0, 0, i))],
        out_specs=[out_blk] * 4,
        out_shape=[shp, shp, jax.ShapeDtypeStruct(shp.shape, BF16), jax.ShapeDtypeStruct(shp.shape, BF16)],
        compiler_params=_cparams(("parallel",)),
        name="peer_select",
    )(st)


def _peer_body(h2t_ref, p1_ref, c1_ref, p2_in_ref, r2_in_ref, u_ref, vt_ref, o_ref, a_ref, wg_ref, p2_ref, r2_ref,
               *, te):
    e = pl.program_id(1)

    @pl.when(e == 0)
    def _():
        o_ref[...] = jnp.zeros(o_ref.shape, F32)
        p2_ref[...] = p2_in_ref[...]
        r2_ref[...] = r2_in_ref[...]

    tm = o_ref.shape[1]
    a_ref[...] = jnp.dot(u_ref[...], h2t_ref[...], preferred_element_type=F32)
    for j in range(te // N_KEYS):
        rows = slice(j * N_KEYS, (j + 1) * N_KEYS)
        for lt in range(tm // LANES):
            lanes = slice(lt * LANES, (lt + 1) * LANES)
            tile = (N_KEYS, LANES)
            w = jnp.zeros(tile, BF16)
            for h in range(PEER_HEADS):
                width = jnp.broadcast_to(c1_ref[h, j:j + 1, lanes], tile).astype(BF16)
                pfirst = jnp.broadcast_to(p1_ref[h, j:j + 1, lanes], tile).astype(BF16)
                keep = r2_ref[h, :, lanes] < width
                w = w + jnp.where(keep, p2_ref[h, :, lanes], jnp.zeros(tile, BF16)) * pfirst
            act = jax.nn.gelu(a_ref[rows, lanes], approximate=True).astype(BF16)
            wg_ref[rows, lanes] = w * act
    o_ref[...] += jnp.dot(vt_ref[...], wg_ref[...], preferred_element_type=F32)


def _peer_experts(h2t, p1, c1, p2, r2, u_bf16, vt_bf16, tm, te):
    T = h2t.shape[1]
    firsts = te // N_KEYS
    sel_all = pl.BlockSpec((PEER_HEADS, N_KEYS, tm), lambda i, e: (0, 0, i))
    sel_blk = pl.BlockSpec((PEER_HEADS, firsts, tm), lambda i, e: (0, e, i))
    return pl.pallas_call(
        functools.partial(_peer_body, te=te),
        grid=(T // tm, N_EXPERTS // te),
        in_specs=[pl.BlockSpec((D_MODEL, tm), lambda i, e: (0, i)), sel_blk, sel_blk, sel_all, sel_all,
                  pl.BlockSpec((te, D_MODEL), lambda i, e: (e, 0)),
                  pl.BlockSpec((D_MODEL, te), lambda i, e: (0, e))],
        out_specs=pl.BlockSpec((D_MODEL, tm), lambda i, e: (0, i)),
        out_shape=jax.ShapeDtypeStruct((D_MODEL, T), F32),
        scratch_shapes=[pltpu.VMEM((te, tm), F32), pltpu.VMEM((te, tm), BF16),
                        pltpu.VMEM((PEER_HEADS, N_KEYS, tm), BF16), pltpu.VMEM((PEER_HEADS, N_KEYS, tm), BF16)],
        compiler_params=_cparams(("parallel", "arbitrary")),
        name="peer_experts",
    )(h2t, p1, c1, p2, r2, u_bf16, vt_bf16)


def _final_body(x1_ref, ot_ref, g_ref, y_ref):
    y_ref[...] = _rms(x1_ref[...] + ot_ref[...].T, g_ref[...])


def _final_norm(x1, out_t, g, tm):
    T = x1.shape[0]
    return pl.pallas_call(
        _final_body,
        grid=(T // tm,),
        in_specs=[pl.BlockSpec((tm, D_MODEL), lambda i: (i, 0)), pl.BlockSpec((D_MODEL, tm), lambda i: (0, i)),
                  pl.BlockSpec((1, D_MODEL), lambda i: (0, 0))],
        out_specs=pl.BlockSpec((tm, D_MODEL), lambda i: (i, 0)),
        out_shape=jax.ShapeDtypeStruct((T, D_MODEL), F32),
        compiler_params=_cparams(("parallel",)),
        name="final_norm",
    )(x1, out_t, g)


TOKEN_TILE = 512
PEER_TOKEN_TILE = 512
PEER_EXPERT_TILE = 2048
CONV_ROWS = 512


def _channel_and_final(attn, c, x, p):
    x1, h2t, st = _out_projection(attn, c, x, p["wo"], p["g2"], p["wq"], p["keys"], TOKEN_TILE)
    p1, c1, p2, r2 = _peer_select(st, TOKEN_TILE)
    out_t = _peer_experts(h2t, p1, c1, p2, r2, p["u"], p["vt"], min(PEER_TOKEN_TILE, x.shape[0]), PEER_EXPERT_TILE)
    return _final_norm(x1, out_t, p["gf"], TOKEN_TILE)


def kernel(x_prompt, x_sample, cache_k, cache_v, state_conv, norm1_g, w_in, conv_w, conv_b, conv_ln_g,
           conv_ln_b, w_out, norm2_g, w_query, sub_keys1, sub_keys2, expert_u, expert_v, final_g):
    bp, s, _ = x_prompt.shape
    bs, t, _ = x_sample.shape
    depth, _, w_buf = cache_k.shape[:3]
    assert depth == 1 and all(s % (d * BAND_BLOCK) == 0 and s // d >= 2 * BAND_BLOCK for _, d in DILATIONS)
    keep = min(DILATIONS[-1][0], s)
    l = 0
    row = lambda a: a.reshape(1, -1)
    p = dict(
        wo=w_out[l].astype(BF16), g2=row(norm2_g[l]), wq=w_query[l].astype(BF16),
        keys=jnp.stack([sub_keys1[l], sub_keys2[l]]).astype(BF16),
        u=expert_u[l].astype(BF16), vt=expert_v[l].T.astype(BF16), gf=row(final_g),
    )
    w_in_b = w_in[l].astype(BF16)
    g1 = row(norm1_g[l])
    cw, cb, lg, lb = conv_w[l], row(conv_b[l]), row(conv_ln_g[l]), row(conv_ln_b[l])

    xp = x_prompt.reshape(bp * s, D_MODEL)
    cos, sin = _rope_tables(jnp.tile(jnp.arange(s), bp))
    q, k, v, g = _in_projection(xp, g1, w_in_b, cos, sin, TOKEN_TILE)
    attn = _attention_prompt(q, k, v, bp, s)
    c = _conv_prompt(g, cw, cb, lg, lb, bp, s, CONV_ROWS)
    y_prompt = _channel_and_final(attn, c, xp, p).reshape(bp, s, D_MODEL)
    new_k_prompt = k.reshape(bp, s, N_HEADS, HEAD_DIM)[None, :, s - keep:]
    new_v_prompt = v.reshape(bp, s, N_HEADS, HEAD_DIM)[None, :, s - keep:]
    new_conv_prompt = g.reshape(bp, s, CONV_CH)[None, :, s - CONV_HIST:]

    xs = x_sample.reshape(bs * t, D_MODEL)
    assert w_buf == min(MAX_WINDOW, PAST_LEN)
    cos, sin = _rope_tables(jnp.tile(PAST_LEN + jnp.arange(t), bs))
    q, k, v, g = _in_projection(xs, g1, w_in_b, cos, sin, TOKEN_TILE)
    heads = lambda a: a.reshape(bs, t, N_HEADS, HEAD_DIM)
    attn = _attention_sample(heads(q), heads(k), heads(v), cache_k[l], cache_v[l])
    attn = attn.reshape(bs * t, ATTN_WIDTH).astype(BF16)
    g_hist = jnp.concatenate([state_conv[l], g.reshape(bs, t, CONV_CH)], axis=1)
    c = _conv_sample(jnp.transpose(g_hist, (1, 0, 2)), cw, cb, lg, lb)
    c = jnp.transpose(c, (1, 0, 2)).reshape(bs * t, CONV_CH)
    y_sample = _channel_and_final(attn, c, xs, p).reshape(bs, t, D_MODEL)
    new_k_sample = k.reshape(bs, t, N_HEADS, HEAD_DIM)[None]
    new_v_sample = v.reshape(bs, t, N_HEADS, HEAD_DIM)[None]
    new_conv_sample = g_hist[None, :, -CONV_HIST:]

    return (y_prompt, y_sample, new_k_prompt, new_v_prompt, new_conv_prompt,
            new_k_sample, new_v_sample, new_conv_sample)
```

```python
import functools

import jax
import jax.numpy as jnp
from jax import lax
from jax.experimental import pallas as pl
from jax.experimental.pallas import tpu as pltpu

F32 = jnp.float32
BF16 = jnp.bfloat16

D_MODEL = 1024
HEAD_DIM = 64
HALF_HEAD = HEAD_DIM // 2
N_HEADS = 12
ATTN_WIDTH = N_HEADS * HEAD_DIM
CONV_CH = D_MODEL - ATTN_WIDTH
CONV_WIDTH = 31
CONV_HIST = CONV_WIDTH - 1
IN_COLS = 3 * ATTN_WIDTH + 2 * CONV_CH
DILATIONS = ((128, 1), (512, 4), (2048, 16))
MAX_WINDOW = 2048
PAST_LEN = 2048
BAND_BLOCK = 128
ROPE_THETA = 10000.0
ATTN_SCALE = HEAD_DIM ** -0.5
NEG_INF = -1e30
N_KEYS = 128
N_EXPERTS = N_KEYS * N_KEYS
PEER_HEADS = 8
PEER_TOPK = 16
D_QUERY = 256
HALF_Q = D_QUERY // 2
RMS_EPS = 1e-6
LN_EPS = 1e-5

LANES = 128
HEADS_PER_LANE_TILE = LANES // HEAD_DIM
N_HEAD_TILES = ATTN_WIDTH // LANES
NOT_SELECTED = 99.0
VMEM_LIMIT = 56 * 1024 * 1024


def _cparams(semantics, flags=None):
    return pltpu.CompilerParams(dimension_semantics=semantics, vmem_limit_bytes=VMEM_LIMIT, flags=flags)


def _rms(x, g):
    return (x * lax.rsqrt(jnp.mean(x * x, axis=-1, keepdims=True) + RMS_EPS)) * g


def _inproj_body(x_ref, g_ref, w_ref, cos_ref, sin_ref, q_ref, k_ref, v_ref, gl_ref):
    hb = _rms(x_ref[...], g_ref[...]).astype(BF16)
    cos = cos_ref[...]
    sin = sin_ref[...]
    lane = lax.broadcasted_iota(jnp.int32, cos.shape, 1)
    first_half = jnp.bitwise_and(lane, HEAD_DIM - 1) < HALF_HEAD

    def rope(t):
        partner = jnp.where(first_half, pltpu.roll(t, LANES - HALF_HEAD, 1), pltpu.roll(t, HALF_HEAD, 1))
        return t * cos + partner * sin

    q = jnp.dot(hb, w_ref[:, 0:ATTN_WIDTH], preferred_element_type=F32)
    k = jnp.dot(hb, w_ref[:, ATTN_WIDTH:2 * ATTN_WIDTH], preferred_element_type=F32)
    for c in range(N_HEAD_TILES):
        sl = slice(c * LANES, (c + 1) * LANES)
        q_ref[:, sl] = rope(q[:, sl]) * ATTN_SCALE
        k_ref[:, sl] = rope(k[:, sl])
    v_ref[...] = jnp.dot(hb, w_ref[:, 2 * ATTN_WIDTH:3 * ATTN_WIDTH], preferred_element_type=F32)
    u = jnp.dot(hb, w_ref[:, 3 * ATTN_WIDTH:IN_COLS], preferred_element_type=F32)
    gl_ref[...] = u[:, :CONV_CH] * jax.nn.sigmoid(u[:, CONV_CH:])


def _in_projection(x, g, w_bf16, cos, sin, tm):
    T = x.shape[0]
    assert cos.shape[0] % tm == 0 and T % cos.shape[0] == 0
    period = cos.shape[0] // tm
    tok = lambda i: (i, 0)
    pos = lambda i: (i % period, 0)
    full = lambda i: (0, 0)
    return pl.pallas_call(
        _inproj_body,
        grid=(T // tm,),
        in_specs=[pl.BlockSpec((tm, D_MODEL), tok), pl.BlockSpec((1, D_MODEL), full),
                  pl.BlockSpec((D_MODEL, IN_COLS), full),
                  pl.BlockSpec((tm, LANES), pos), pl.BlockSpec((tm, LANES), pos)],
        out_specs=[pl.BlockSpec((tm, ATTN_WIDTH), tok)] * 3 + [pl.BlockSpec((tm, CONV_CH), tok)],
        out_shape=[jax.ShapeDtypeStruct((T, ATTN_WIDTH), F32)] * 3 + [jax.ShapeDtypeStruct((T, CONV_CH), F32)],
        compiler_params=_cparams(("parallel",)),
        name="in_projection",
    )(x, g, w_bf16, cos, sin)


def _rope_tables(pos):
    inv = jnp.power(ROPE_THETA, -jnp.arange(HALF_HEAD, dtype=F32) * (2.0 / HEAD_DIM))
    ang = pos.astype(F32)[:, None] * inv[None, :]
    cos, sin = jnp.cos(ang), jnp.sin(ang)
    return (jnp.concatenate([cos, cos] * HEADS_PER_LANE_TILE, axis=-1),
            jnp.concatenate([-sin, sin] * HEADS_PER_LANE_TILE, axis=-1))


ATTN_MERGE_ROWS = 256


def _attn_prompt_body(q_ref, k_ref, v_ref, o_ref, ob_ref, mb_ref, lb_ref, *, seq):
    Q = BAND_BLOCK
    lane = lax.broadcasted_iota(jnp.int32, (Q, LANES), 1)
    head_lanes = [lane < HEAD_DIM, lane >= HEAD_DIM]
    q_minus_k = (lax.broadcasted_iota(jnp.int32, (Q, 2 * Q), 0)
                 - lax.broadcasted_iota(jnp.int32, (Q, 2 * Q), 1))

    for b, (win, dil) in enumerate(DILATIONS):
        reach = win // dil
        n_blocks = (seq // dil) // Q

        def rows(start, size, dil=dil):
            return pl.ds(start, size) if dil == 1 else pl.ds(start, size, stride=dil)

        def step(idx, carry, b=b, dil=dil, reach=reach, n_blocks=n_blocks, rows=rows):
            r = idx // n_blocks
            a0 = (idx % n_blocks) * Q
            ks = jnp.maximum(a0 - Q, 0)
            q_rows = rows(r + a0 * dil, Q)
            k_rows = rows(r + ks * dil, 2 * Q)
            q = q_ref[q_rows, :]
            k = k_ref[k_rows, :].astype(BF16)
            v = v_ref[k_rows, :].astype(BF16)
            off = a0 - ks
            valid = (q_minus_k >= -off) & (q_minus_k <= reach - off)
            o_full = jnp.zeros((Q, LANES), F32)
            m_full = jnp.zeros((Q, LANES), F32)
            l_full = jnp.zeros((Q, LANES), F32)
            for h in range(HEADS_PER_LANE_TILE):
                qh = jnp.where(head_lanes[h], q, 0.0).astype(BF16)
                s = lax.dot_general(qh, k, (((1,), (1,)), ((), ())), preferred_element_type=F32)
                s = jnp.where(valid, s, NEG_INF)
                m = jnp.max(s, axis=1, keepdims=True)
                p = jnp.exp(s - m)
                l = jnp.sum(p, axis=1, keepdims=True)
                pv = jnp.dot(p.astype(BF16), v, preferred_element_type=F32)
                o_full = jnp.where(head_lanes[h], pv, o_full)
                m_full = jnp.where(head_lanes[h], m, m_full)
                l_full = jnp.where(head_lanes[h], l, l_full)
            ob_ref[b, q_rows, :] = o_full
            mb_ref[b, q_rows, :] = m_full
            lb_ref[b, q_rows, :] = l_full
            return carry

        lax.fori_loop(0, dil * n_blocks, step, 0, unroll=8)

    def merge(i, carry):
        rws = pl.ds(pl.multiple_of(i * ATTN_MERGE_ROWS, ATTN_MERGE_ROWS), ATTN_MERGE_ROWS)
        n_br = len(DILATIONS)
        m_all = mb_ref[0, rws, :]
        for b in range(1, n_br):
            m_all = jnp.maximum(m_all, mb_ref[b, rws, :])
        num = jnp.zeros((ATTN_MERGE_ROWS, LANES), F32)
        den = jnp.zeros((ATTN_MERGE_ROWS, LANES), F32)
        for b in range(n_br):
            w = jnp.exp(mb_ref[b, rws, :] - m_all)
            num = num + w * ob_ref[b, rws, :]
            den = den + w * lb_ref[b, rws, :]
        o_ref[rws, :] = (num / den).astype(o_ref.dtype)
        return carry

    lax.fori_loop(0, seq // ATTN_MERGE_ROWS, merge, 0)


def _attention_prompt(q, k, v, batch, seq):
    T = q.shape[0]
    blk = pl.BlockSpec((seq, LANES), lambda b, hp: (b, hp))
    return pl.pallas_call(
        functools.partial(_attn_prompt_body, seq=seq),
        grid=(batch, N_HEAD_TILES),
        in_specs=[blk, blk, blk],
        out_specs=blk,
        out_shape=jax.ShapeDtypeStruct((T, ATTN_WIDTH), BF16),
        scratch_shapes=[pltpu.VMEM((len(DILATIONS), seq, LANES), F32)] * 3,
        compiler_params=_cparams(("parallel", "parallel")),
        name="attention_prompt",
    )(q, k, v)


SAMPLE_CHUNK = 128


def _attn_sample_body(q_ref, kn_ref, vn_ref, kt_ref, vt_ref, kg_ref, vg_ref, o_ref, s_ref, p_ref, m_ref, l_ref,
                      acc_ref, *, t_new, tail, past):
    c = pl.program_id(1)
    n_seq = q_ref.shape[-1]
    (w1, d1), (w4, d4), (w16, d16) = DILATIONS
    q = [q_ref[i] for i in range(t_new)]

    @pl.when(c == 0)
    def _():
        for i in range(t_new):
            s = [jnp.sum(q[i] * kn_ref[j], axis=0, keepdims=True) for j in range(i + 1)]
            m = s[0]
            for j in range(1, i + 1):
                m = jnp.maximum(m, s[j])
            l = jnp.zeros((1, n_seq), F32)
            acc = jnp.zeros((HEAD_DIM, n_seq), F32)
            for j in range(i + 1):
                p = (float(len(DILATIONS)) if j == i else 1.0) * jnp.exp(s[j] - m)
                l = l + p
                acc = acc + p * vn_ref[j]
            m_ref[i] = m
            l_ref[i] = l
            acc_ref[i] = acc

    def chunk(k_row_of, v_row_of):
        def score_row(r, carry):
            k_row = k_row_of(r)
            for i in range(t_new):
                s_ref[i, pl.ds(r, 1), :] = jnp.sum(q[i] * k_row, axis=0, keepdims=True)
            return carry

        lax.fori_loop(0, SAMPLE_CHUNK, score_row, 0, unroll=2)

        slot = c * SAMPLE_CHUNK + lax.broadcasted_iota(jnp.int32, (SAMPLE_CHUNK, n_seq), 0)
        in_tail = slot < tail
        cache_row = slot + (past - tail)
        grp = jnp.right_shift(slot - tail, shift)
        grp_row = jnp.bitwise_and(slot - tail, low)
        for i in range(t_new):
            in_d1 = in_tail & (cache_row >= past + i - w1 // d1)
            in_d4 = in_tail & (jnp.bitwise_and(cache_row - i, d4 - 1) == 0) & (cache_row >= past + i - w4)
            in_d16 = (~in_tail) & (grp_row == i) & (grp >= past // d16 - w16 // d16)
            mult = in_d1.astype(F32) + in_d4.astype(F32) + in_d16.astype(F32)
            s = jnp.where(mult > 0, s_ref[i], NEG_INF)
            m_old = m_ref[i]
            m_new = jnp.maximum(m_old, jnp.max(s, axis=0, keepdims=True))
            alpha = jnp.exp(m_old - m_new)
            p = mult * jnp.exp(s - m_new)
            p_ref[i] = p
            l_ref[i] = alpha * l_ref[i] + jnp.sum(p, axis=0, keepdims=True)
            m_ref[i] = m_new
            acc_ref[i] = acc_ref[i] * alpha

        def value_row(r, acc):
            v_row = v_row_of(r)
            return tuple(acc[i] + p_ref[i, pl.ds(r, 1), :] * v_row for i in range(t_new))

        acc = lax.fori_loop(0, SAMPLE_CHUNK, value_row, tuple(acc_ref[i] for i in range(t_new)), unroll=2)
        for i in range(t_new):
            acc_ref[i] = acc[i]

    shift, low = t_new.bit_length() - 1, t_new - 1
    n_groups = SAMPLE_CHUNK // t_new

    def strided_chunk(k_row_of, v_row_of, in_tail):
        def score_group(g, carry):
            for i in range(t_new):
                s_ref[i, pl.ds(g, 1), :] = jnp.sum(q[i] * k_row_of(g, i), axis=0, keepdims=True)
            return carry

        lax.fori_loop(0, n_groups, score_group, 0, unroll=2)

        grp_in_chunk = lax.broadcasted_iota(jnp.int32, (n_groups, n_seq), 0)
        for i in range(t_new):
            slot = c * SAMPLE_CHUNK + grp_in_chunk * t_new + i
            if in_tail:
                member = slot + (past - tail) >= past + i - w4
            else:
                member = jnp.right_shift(slot - tail, shift) >= past // d16 - w16 // d16
            s = jnp.where(member, s_ref[i, 0:n_groups, :], NEG_INF)
            m_old = m_ref[i]
            m_new = jnp.maximum(m_old, jnp.max(s, axis=0, keepdims=True))
            alpha = jnp.exp(m_old - m_new)
            p = jnp.where(member, jnp.exp(s - m_new), 0.0)
            p_ref[i, 0:n_groups, :] = p
            l_ref[i] = alpha * l_ref[i] + jnp.sum(p, axis=0, keepdims=True)
            m_ref[i] = m_new
            acc_ref[i] = acc_ref[i] * alpha

        def value_group(g, acc):
            return tuple(acc[i] + p_ref[i, pl.ds(g, 1), :] * v_row_of(g, i) for i in range(t_new))

        acc = lax.fori_loop(0, n_groups, value_group, tuple(acc_ref[i] for i in range(t_new)), unroll=2)
        for i in range(t_new):
            acc_ref[i] = acc[i]

    tail_chunks = tail // SAMPLE_CHUNK
    dense_chunks = -(-(w1 // d1) // SAMPLE_CHUNK)

    @pl.when(c < tail_chunks - dense_chunks)
    def _():
        strided_chunk(lambda g, i: kt_ref[g * t_new + i], lambda g, i: vt_ref[g * t_new + i], True)

    @pl.when((c >= tail_chunks - dense_chunks) & (c < tail_chunks))
    def _():
        chunk(lambda r: kt_ref[r], lambda r: vt_ref[r])

    @pl.when(c >= tail_chunks)
    def _():
        strided_chunk(lambda g, i: kg_ref[g, i], lambda g, i: vg_ref[g, i], False)

    @pl.when(c == pl.num_programs(1) - 1)
    def _():
        for i in range(t_new):
            o_ref[i] = acc_ref[i] / l_ref[i]


def _attention_sample(q, k_new, v_new, cache_k, cache_v):
    n_seq, t_new = q.shape[:2]
    past = cache_k.shape[1]
    (w1, d1), (w4, d4), (w16, d16) = DILATIONS
    tail = w4
    assert past % d16 == 0 and past >= w16 and w1 <= tail and t_new & (t_new - 1) == 0
    assert t_new == d4 and (past - tail) % d4 == 0 and d1 == 1
    tail_chunks = tail // SAMPLE_CHUNK
    group_chunks = (past // d16 * t_new) // SAMPLE_CHUNK
    groups_per_chunk = SAMPLE_CHUNK // t_new
    assert tail % SAMPLE_CHUNK == 0 and past % SAMPLE_CHUNK == 0 and (past // d16) % groups_per_chunk == 0
    lanes_last = lambda a: jnp.transpose(a, (1, 2, 3, 0))
    grouped = lambda a: a.reshape((past // d16, d16) + a.shape[1:])
    ck, cv = lanes_last(cache_k), lanes_last(cache_v)
    new_blk = pl.BlockSpec((t_new, None, HEAD_DIM, n_seq), lambda h, c: (0, h, 0, 0))
    tail_blk = pl.BlockSpec((SAMPLE_CHUNK, None, HEAD_DIM, n_seq),
                            lambda h, c: (past // SAMPLE_CHUNK - tail_chunks + jnp.minimum(c, tail_chunks - 1), h, 0, 0))
    group_blk = pl.BlockSpec((groups_per_chunk, t_new, None, HEAD_DIM, n_seq),
                             lambda h, c: (jnp.maximum(c - tail_chunks, 0), 0, h, 0, 0))
    out = pl.pallas_call(
        functools.partial(_attn_sample_body, t_new=t_new, tail=tail, past=past),
        grid=(N_HEADS, tail_chunks + group_chunks),
        in_specs=[new_blk, new_blk, new_blk, tail_blk, tail_blk, group_blk, group_blk],
        out_specs=pl.BlockSpec((None, t_new, HEAD_DIM, n_seq), lambda h, c: (h, 0, 0, 0)),
        out_shape=jax.ShapeDtypeStruct((N_HEADS, t_new, HEAD_DIM, n_seq), F32),
        scratch_shapes=[pltpu.VMEM((t_new, SAMPLE_CHUNK, n_seq), F32), pltpu.VMEM((t_new, SAMPLE_CHUNK, n_seq), F32),
                        pltpu.VMEM((t_new, 1, n_seq), F32), pltpu.VMEM((t_new, 1, n_seq), F32),
                        pltpu.VMEM((t_new, HEAD_DIM, n_seq), F32)],
        compiler_params=_cparams(("parallel", "arbitrary")),
        name="attention_sample",
    )(lanes_last(q), lanes_last(k_new), lanes_last(v_new), ck, cv, grouped(ck), grouped(cv))
    return jnp.transpose(out, (3, 1, 0, 2)).reshape(n_seq, t_new, ATTN_WIDTH)


def _ln_swish(y, lg, lb):
    mu = jnp.mean(y, axis=-1, keepdims=True)
    var = jnp.mean(jnp.square(y - mu), axis=-1, keepdims=True)
    yn = (y - mu) * lax.rsqrt(var + LN_EPS) * lg + lb
    return yn * jax.nn.sigmoid(yn)


def _conv_prompt_body(cur_ref, prev_ref, cw_ref, cb_ref, lg_ref, lb_ref, o_ref, win_ref, *, ch):
    j = pl.program_id(1)
    win_ref[0:ch, :] = jnp.where(j == 0, 0.0, prev_ref[...])
    win_ref[ch:2 * ch, :] = cur_ref[...]
    acc = jnp.zeros((ch, CONV_CH), F32)
    for w in range(CONV_WIDTH):
        acc = acc + win_ref[ch - CONV_HIST + w:2 * ch - CONV_HIST + w, :] * cw_ref[w:w + 1, :]
    o_ref[...] = _ln_swish(acc + cb_ref[...], lg_ref[...], lb_ref[...]).astype(o_ref.dtype)


def _conv_prompt(g, cw, cb, lg, lb, batch, seq, ch):
    nj = seq // ch
    par = pl.BlockSpec((1, CONV_CH), lambda b, j: (0, 0))
    return pl.pallas_call(
        functools.partial(_conv_prompt_body, ch=ch),
        grid=(batch, nj),
        in_specs=[pl.BlockSpec((ch, CONV_CH), lambda b, j: (b * nj + j, 0)),
                  pl.BlockSpec((ch, CONV_CH), lambda b, j: (b * nj + jnp.maximum(j - 1, 0), 0)),
                  pl.BlockSpec((CONV_WIDTH, CONV_CH), lambda b, j: (0, 0)), par, par, par],
        out_specs=pl.BlockSpec((ch, CONV_CH), lambda b, j: (b * nj + j, 0)),
        out_shape=jax.ShapeDtypeStruct((batch * seq, CONV_CH), BF16),
        scratch_shapes=[pltpu.VMEM((2 * ch, CONV_CH), F32)],
        compiler_params=_cparams(("parallel", "parallel")),
        name="conv_prompt",
    )(g, g, cw, cb, lg, lb)


def _conv_sample_body(gh_ref, cw_ref, cb_ref, lg_ref, lb_ref, o_ref, *, t_new):
    for i in range(t_new):
        acc = jnp.zeros(gh_ref.shape[1:], F32)
        for w in range(CONV_WIDTH):
            acc = acc + gh_ref[i + w] * cw_ref[w:w + 1, :]
        o_ref[i] = _ln_swish(acc + cb_ref[...], lg_ref[...], lb_ref[...]).astype(o_ref.dtype)


def _conv_sample(gh_t, cw, cb, lg, lb):
    lh, n_seq, _ = gh_t.shape
    t_new = lh - CONV_HIST
    return pl.pallas_call(
        functools.partial(_conv_sample_body, t_new=t_new),
        out_shape=jax.ShapeDtypeStruct((t_new, n_seq, CONV_CH), BF16),
        name="conv_sample",
    )(gh_t, cw, cb, lg, lb)


def _outproj_body(attn_ref, c_ref, x_ref, wo_ref, g2_ref, wq_ref, keys_ref, x1_ref, h2t_ref, st_ref):
    a = jnp.dot(attn_ref[...], wo_ref[0:ATTN_WIDTH, :], preferred_element_type=F32)
    a = a + jnp.dot(c_ref[...], wo_ref[ATTN_WIDTH:D_MODEL, :], preferred_element_type=F32)
    x1 = x_ref[...] + a
    x1_ref[...] = x1
    h2 = _rms(x1, g2_ref[...])
    h2t_ref[...] = h2.T.astype(BF16)
    qv = jnp.dot(h2.astype(BF16), wq_ref[...], preferred_element_type=F32).astype(BF16)
    nt = (((1,), (1,)), ((), ()))
    for h in range(PEER_HEADS):
        for side in range(2):
            qs = qv[:, h * D_QUERY + side * HALF_Q:h * D_QUERY + (side + 1) * HALF_Q]
            st_ref[2 * h + side] = lax.dot_general(keys_ref[side], qs, nt, preferred_element_type=F32)


def _out_projection(attn, c, x, wo_bf16, g2, wq_bf16, keys_bf16, tm):
    T = x.shape[0]
    tok = lambda i: (i, 0)
    full2 = lambda i: (0, 0)
    return pl.pallas_call(
        _outproj_body,
        grid=(T // tm,),
        in_specs=[pl.BlockSpec((tm, ATTN_WIDTH), tok), pl.BlockSpec((tm, CONV_CH), tok),
                  pl.BlockSpec((tm, D_MODEL), tok), pl.BlockSpec((D_MODEL, D_MODEL), full2),
                  pl.BlockSpec((1, D_MODEL), full2), pl.BlockSpec((D_MODEL, PEER_HEADS * D_QUERY), full2),
                  pl.BlockSpec((2, N_KEYS, HALF_Q), lambda i: (0, 0, 0))],
        out_specs=[pl.BlockSpec((tm, D_MODEL), tok), pl.BlockSpec((D_MODEL, tm), lambda i: (0, i)),
                   pl.BlockSpec((2 * PEER_HEADS, N_KEYS, tm), lambda i: (0, 0, i))],
        out_shape=[jax.ShapeDtypeStruct((T, D_MODEL), F32), jax.ShapeDtypeStruct((D_MODEL, T), BF16),
                   jax.ShapeDtypeStruct((2 * PEER_HEADS, N_KEYS, T), F32)],
        compiler_params=_cparams(("parallel",)),
        name="out_projection",
    )(attn, c, x, wo_bf16, g2, wq_bf16, keys_bf16)


def _top16(s, order):
    rank = jnp.full(s.shape, NOT_SELECTED, F32)
    slot = lax.broadcasted_iota(jnp.int32, (PEER_TOPK, s.shape[1]), 0)
    vals = jnp.zeros((PEER_TOPK, s.shape[1]), F32)
    for a in range(PEER_TOPK):
        mx = jnp.max(s, axis=0, keepdims=True)
        first = jnp.min(jnp.where(s == mx, order, 1e9), axis=0, keepdims=True)
        sel = order == first
        rank = jnp.where(sel, float(a), rank)
        s = jnp.where(sel, -jnp.inf, s)
        vals = jnp.where(slot == a, mx, vals)
    return rank, vals


def _cex(a, b):
    if a is None or b is None:
        return (b if a is None else a), None
    return jnp.maximum(a, b), jnp.minimum(a, b)


def _bitonic_sort_desc(xs):
    xs, n, k = list(xs), len(xs), 2
    while k <= n:
        j = k // 2
        while j >= 1:
            for i in range(n):
                if i ^ j > i:
                    hi, lo = _cex(xs[i], xs[i ^ j])
                    xs[i], xs[i ^ j] = (hi, lo) if (i & k) == 0 else (lo, hi)
            j //= 2
        k *= 2
    return xs


def _bitonic_merge_desc(xs):
    xs, j = list(xs), len(xs) // 2
    while j >= 1:
        for i in range(len(xs)):
            if (i & j) == 0:
                xs[i], xs[i + j] = _cex(xs[i], xs[i + j])
        j //= 2
    return xs


def _top16_values(slabs):
    K = PEER_TOPK
    xs = _bitonic_sort_desc(list(slabs) + [None] * (K - len(slabs)))
    for shift in (4, 2, 1):
        other = [None if x is None else pltpu.roll(x, shift, 0) for x in xs]
        xs = _bitonic_merge_desc([_cex(xs[i], other[K - 1 - i])[0] for i in range(K)])
    return xs


def _search16(test, v):
    sel = jnp.where
    c1 = test(v[7])
    c2 = test(sel(c1, v[11], v[3]))
    c3 = test(sel(c1, sel(c2, v[13], v[9]), sel(c2, v[5], v[1])))
    c4 = test(sel(c1, sel(c2, sel(c3, v[14], v[12]), sel(c3, v[10], v[8])),
                  sel(c2, sel(c3, v[6], v[4]), sel(c3, v[2], v[0]))))
    return sel(c1, 8.0, 0.0) + sel(c2, 4.0, 0.0) + sel(c3, 2.0, 0.0) + sel(c4, 1.0, 0.0)


def _rows_sum(x):
    for shift in (4, 2, 1):
        x = x + pltpu.roll(x, shift, 0)
    return x


def _select_fast(s1, s2):
    K, R = PEER_TOPK, 8
    x1 = [s1[R * i:R * (i + 1)] for i in range(N_KEYS // R)]
    x2 = [s2[R * i:R * (i + 1)] for i in range(N_KEYS // R)]
    v1, v2 = _top16_values(x1), _top16_values(x2)
    row = lax.broadcasted_iota(jnp.int32, x1[0].shape, 0)

    def rows_of(vals):
        out = vals[0]
        for b in range(1, R):
            out = jnp.where(row == b, vals[b], out)
        return out

    v2_lo, v2_hi, v1_hi = rows_of(v2[0:R]), rows_of(v2[R:K]), rows_of(v1[R:K])
    cand = [v1[0] + v2_lo, v1[0] + v2_hi] + [v1[a] + v2_lo for a in range(1, R)] + [v1_hi + v2[0]]
    cs = _top16_values(cand)
    tau = cs[K - 1]
    z = jnp.exp(cs[0] - cs[0])
    for a in range(1, K):
        z = z + jnp.exp(cs[a] - cs[0])

    def count_ge(xs, t):
        n = jnp.zeros(xs[0].shape, F32)
        for x in xs:
            n = n + jnp.where(x >= t, 1.0, 0.0)
        return _rows_sum(n)

    tie = (count_ge(x1, v1[K - 1]) != K) | (count_ge(x2, v2[K - 1]) != K) | (count_ge(cand, tau) != K)
    for a in range(K - 1):
        tie = tie | (v1[a] == v1[a + 1]) | (v2[a] == v2[a + 1])

    p1, width, p2, r2 = [], [], [], []
    for x in x1:
        in1 = x >= v1[K - 1]
        wid = _search16(lambda vb, x=x: (x + vb) >= tau, v2) + jnp.where((x + v2[K - 1]) >= tau, 1.0, 0.0)
        p1.append(jnp.where(in1, jnp.exp(x - v1[0]) / z, 0.0))
        width.append(jnp.where(in1, wid, 0.0))
    for x in x2:
        in2 = x >= v2[K - 1]
        p2.append(jnp.where(in2, jnp.exp(x - v2[0]), 0.0))
        r2.append(jnp.where(in2, _search16(lambda vb, x=x: vb > x, v2), NOT_SELECTED))
    cat = lambda parts: jnp.concatenate(parts, axis=0)
    return cat(p1), cat(width), cat(p2), cat(r2), tie


def _select_exact(s1, s2, key_order, cand_order):
    K = PEER_TOPK
    half = K // 2
    slot16 = lax.broadcasted_iota(jnp.int32, (K, s1.shape[1]), 0)
    rank1, v1 = _top16(s1, key_order)
    rank2, v2 = _top16(s2, key_order)
    cand = [v1[0:1, :] + v2]
    cand += [v1[a:a + 1, :] + v2[0:half, :] for a in range(1, half)]
    cand += [v1[half:K, :] + v2[0:1, :]]
    cand = jnp.concatenate(cand, axis=0)
    crank, _ = _top16(cand, cand_order)
    chosen = crank < float(K)
    top = v1[0:1, :] + v2[0:1, :]
    z = jnp.sum(jnp.where(chosen, jnp.exp(cand - top), 0.0), axis=0, keepdims=True)
    chosen_f = chosen.astype(F32)
    count = jnp.zeros((K, s1.shape[1]), F32)
    count = jnp.where(slot16 == 0, jnp.sum(chosen_f[0:K], axis=0, keepdims=True), count)
    for a in range(1, half):
        lo = K + (a - 1) * half
        count = jnp.where(slot16 == a, jnp.sum(chosen_f[lo:lo + half], axis=0, keepdims=True), count)
    count = jnp.concatenate([count[0:half], chosen_f[K + (half - 1) * half:]], axis=0)
    width = jnp.zeros(s1.shape, F32)
    for a in range(K):
        width = jnp.where(rank1 == float(a), count[a:a + 1, :], width)
    in1 = rank1 < float(K)
    in2 = rank2 < float(K)
    return (jnp.where(in1, jnp.exp(s1 - v1[0:1, :]) / z, 0.0), width,
            jnp.where(in2, jnp.exp(s2 - v2[0:1, :]), 0.0), rank2)


def _peer_select_body(st_ref, p1_ref, c1_ref, p2_ref, r2_ref, *, tl):
    K = PEER_TOPK
    half = K // 2
    key_order = lax.broadcasted_iota(jnp.int32, (N_KEYS, LANES), 0).astype(F32)
    sub = lax.broadcasted_iota(jnp.int32, (half, LANES), 0)
    flat = [lax.broadcasted_iota(jnp.int32, (K, LANES), 0)]
    flat += [a * K + sub for a in range(1, half)]
    flat += [(half + sub) * K]
    cand_order = jnp.concatenate(flat, axis=0).astype(F32)

    def one(idx, carry):
        h = idx // (tl // LANES)
        ls = pl.multiple_of((idx % (tl // LANES)) * LANES, LANES)
        lanes = pl.ds(ls, LANES)
        s1 = st_ref[2 * h, :, lanes]
        s2 = st_ref[2 * h + 1, :, lanes]

        def write(p1, width, p2, r2):
            p1_ref[h, :, lanes] = p1
            c1_ref[h, :, lanes] = width
            p2_ref[h, :, lanes] = p2.astype(p2_ref.dtype)
            r2_ref[h, :, lanes] = r2.astype(r2_ref.dtype)

        *fast, tie = _select_fast(s1, s2)
        write(*fast)

        @pl.when(jnp.max(jnp.where(tie, 1.0, 0.0)) > 0.0)
        def _():
            write(*_select_exact(s1, s2, key_order, cand_order))

        return carry

    lax.fori_loop(0, PEER_HEADS * (tl // LANES), one, 0)


def _peer_select(st, tl):
    T = st.shape[2]
    out_blk = pl.BlockSpec((PEER_HEADS, N_KEYS, tl), lambda i: (0, 0, i))
    shp = jax.ShapeDtypeStruct((PEER_HEADS, N_KEYS, T), F32)
    return pl.pallas_call(
        functools.partial(_peer_select_body, tl=tl),
        grid=(T // tl,),
        in_specs=[pl.BlockSpec((2 * PEER_HEADS, N_KEYS, tl), lambda i: (0, 0, i))],
        out_specs=[out_blk] * 4,
        out_shape=[shp, shp, jax.ShapeDtypeStruct(shp.shape, BF16), jax.ShapeDtypeStruct(shp.shape, BF16)],
        compiler_params=_cparams(("parallel",)),
        name="peer_select",
    )(st)


def _peer_body(h2t_ref, p1_ref, c1_ref, p2_in_ref, r2_in_ref, u_ref, vt_ref, o_ref, a_ref, wg_ref, p2_ref, r2_ref,
               *, te):
    e = pl.program_id(1)

    @pl.when(e == 0)
    def _():
        o_ref[...] = jnp.zeros(o_ref.shape, F32)
        p2_ref[...] = p2_in_ref[...]
        r2_ref[...] = r2_in_ref[...]

    tm = o_ref.shape[1]
    a_ref[...] = jnp.dot(u_ref[...], h2t_ref[...], preferred_element_type=F32)
    for j in range(te // N_KEYS):
        rows = slice(j * N_KEYS, (j + 1) * N_KEYS)
        for lt in range(tm // LANES):
            lanes = slice(lt * LANES, (lt + 1) * LANES)
            tile = (N_KEYS, LANES)
            w = jnp.zeros(tile, BF16)
            for h in range(PEER_HEADS):
                width = jnp.broadcast_to(c1_ref[h, j:j + 1, lanes], tile).astype(BF16)
                pfirst = jnp.broadcast_to(p1_ref[h, j:j + 1, lanes], tile).astype(BF16)
                keep = r2_ref[h, :, lanes] < width
                w = w + jnp.where(keep, p2_ref[h, :, lanes], jnp.zeros(tile, BF16)) * pfirst
            act = jax.nn.gelu(a_ref[rows, lanes], approximate=True).astype(BF16)
            wg_ref[rows, lanes] = w * act
    o_ref[...] += jnp.dot(vt_ref[...], wg_ref[...], preferred_element_type=F32)


def _peer_experts(h2t, p1, c1, p2, r2, u_bf16, vt_bf16, tm, te):
    T = h2t.shape[1]
    firsts = te // N_KEYS
    sel_all = pl.BlockSpec((PEER_HEADS, N_KEYS, tm), lambda i, e: (0, 0, i))
    sel_blk = pl.BlockSpec((PEER_HEADS, firsts, tm), lambda i, e: (0, e, i))
    return pl.pallas_call(
        functools.partial(_peer_body, te=te),
        grid=(T // tm, N_EXPERTS // te),
        in_specs=[pl.BlockSpec((D_MODEL, tm), lambda i, e: (0, i)), sel_blk, sel_blk, sel_all, sel_all,
                  pl.BlockSpec((te, D_MODEL), lambda i, e: (e, 0)),
                  pl.BlockSpec((D_MODEL, te), lambda i, e: (0, e))],
        out_specs=pl.BlockSpec((D_MODEL, tm), lambda i, e: (0, i)),
        out_shape=jax.ShapeDtypeStruct((D_MODEL, T), F32),
        scratch_shapes=[pltpu.VMEM((te, tm), F32), pltpu.VMEM((te, tm), BF16),
                        pltpu.VMEM((PEER_HEADS, N_KEYS, tm), BF16), pltpu.VMEM((PEER_HEADS, N_KEYS, tm), BF16)],
        compiler_params=_cparams(("parallel", "arbitrary")),
        name="peer_experts",
    )(h2t, p1, c1, p2, r2, u_bf16, vt_bf16)


def _final_body(x1_ref, ot_ref, g_ref, y_ref):
    y_ref[...] = _rms(x1_ref[...] + ot_ref[...].T, g_ref[...])


def _final_norm(x1, out_t, g, tm):
    T = x1.shape[0]
    return pl.pallas_call(
        _final_body,
        grid=(T // tm,),
        in_specs=[pl.BlockSpec((tm, D_MODEL), lambda i: (i, 0)), pl.BlockSpec((D_MODEL, tm), lambda i: (0, i)),
                  pl.BlockSpec((1, D_MODEL), lambda i: (0, 0))],
        out_specs=pl.BlockSpec((tm, D_MODEL), lambda i: (i, 0)),
        out_shape=jax.ShapeDtypeStruct((T, D_MODEL), F32),
        compiler_params=_cparams(("parallel",)),
        name="final_norm",
    )(x1, out_t, g)


TOKEN_TILE = 512
PEER_TOKEN_TILE = 512
PEER_EXPERT_TILE = 2048
CONV_ROWS = 512


def _channel_and_final(attn, c, x, p):
    x1, h2t, st = _out_projection(attn, c, x, p["wo"], p["g2"], p["wq"], p["keys"], TOKEN_TILE)
    p1, c1, p2, r2 = _peer_select(st, TOKEN_TILE)
    out_t = _peer_experts(h2t, p1, c1, p2, r2, p["u"], p["vt"], min(PEER_TOKEN_TILE, x.shape[0]), PEER_EXPERT_TILE)
    return _final_norm(x1, out_t, p["gf"], TOKEN_TILE)


def kernel(x_prompt, x_sample, cache_k, cache_v, state_conv, norm1_g, w_in, conv_w, conv_b, conv_ln_g,
           conv_ln_b, w_out, norm2_g, w_query, sub_keys1, sub_keys2, expert_u, expert_v, final_g):
    bp, s, _ = x_prompt.shape
    bs, t, _ = x_sample.shape
    depth, _, w_buf = cache_k.shape[:3]
    assert depth == 1 and all(s % (d * BAND_BLOCK) == 0 and s // d >= 2 * BAND_BLOCK for _, d in DILATIONS)
    keep = min(DILATIONS[-1][0], s)
    l = 0
    row = lambda a: a.reshape(1, -1)
    p = dict(
        wo=w_out[l].astype(BF16), g2=row(norm2_g[l]), wq=w_query[l].astype(BF16),
        keys=jnp.stack([sub_keys1[l], sub_keys2[l]]).astype(BF16),
        u=expert_u[l].astype(BF16), vt=expert_v[l].T.astype(BF16), gf=row(final_g),
    )
    w_in_b = w_in[l].astype(BF16)
    g1 = row(norm1_g[l])
    cw, cb, lg, lb = conv_w[l], row(conv_b[l]), row(conv_ln_g[l]), row(conv_ln_b[l])

    xp = x_prompt.reshape(bp * s, D_MODEL)
    cos, sin = _rope_tables(jnp.arange(s))
    q, k, v, g = _in_projection(xp, g1, w_in_b, cos, sin, TOKEN_TILE)
    attn = _attention_prompt(q, k, v, bp, s)
    c = _conv_prompt(g, cw, cb, lg, lb, bp, s, CONV_ROWS)
    y_prompt = _channel_and_final(attn, c, xp, p).reshape(bp, s, D_MODEL)
    kept = lambda a: a.reshape(bp, s, ATTN_WIDTH)[:, s - keep:].reshape(1, bp, keep, N_HEADS, HEAD_DIM)
    new_k_prompt, new_v_prompt = kept(k), kept(v)
    new_conv_prompt = g.reshape(bp, s, CONV_CH)[None, :, s - CONV_HIST:]

    xs = x_sample.reshape(bs * t, D_MODEL)
    assert w_buf == min(MAX_WINDOW, PAST_LEN)
    cos, sin = _rope_tables(jnp.tile(PAST_LEN + jnp.arange(t), bs))
    q, k, v, g = _in_projection(xs, g1, w_in_b, cos, sin, TOKEN_TILE)
    heads = lambda a: a.reshape(bs, t, N_HEADS, HEAD_DIM)
    attn = _attention_sample(heads(q), heads(k), heads(v), cache_k[l], cache_v[l])
    attn = attn.reshape(bs * t, ATTN_WIDTH).astype(BF16)
    g_hist = jnp.concatenate([state_conv[l], g.reshape(bs, t, CONV_CH)], axis=1)
    c = _conv_sample(jnp.transpose(g_hist, (1, 0, 2)), cw, cb, lg, lb)
    c = jnp.transpose(c, (1, 0, 2)).reshape(bs * t, CONV_CH)
    y_sample = _channel_and_final(attn, c, xs, p).reshape(bs, t, D_MODEL)
    new_k_sample = k.reshape(bs, t, N_HEADS, HEAD_DIM)[None]
    new_v_sample = v.reshape(bs, t, N_HEADS, HEAD_DIM)[None]
    new_conv_sample = g_hist[None, :, -CONV_HIST:]

    return (y_prompt, y_sample, new_k_prompt, new_v_prompt, new_conv_prompt,
            new_k_sample, new_v_sample, new_conv_sample)
```

```python
import functools

import jax
import jax.numpy as jnp
from jax import lax
from jax.experimental import pallas as pl
from jax.experimental.pallas import tpu as pltpu

F32 = jnp.float32
BF16 = jnp.bfloat16

D_MODEL = 1024
HEAD_DIM = 64
HALF_HEAD = HEAD_DIM // 2
N_HEADS = 12
ATTN_WIDTH = N_HEADS * HEAD_DIM
CONV_CH = D_MODEL - ATTN_WIDTH
CONV_WIDTH = 31
CONV_HIST = CONV_WIDTH - 1
IN_COLS = 3 * ATTN_WIDTH + 2 * CONV_CH
DILATIONS = ((128, 1), (512, 4), (2048, 16))
MAX_WINDOW = 2048
PAST_LEN = 2048
BAND_BLOCK = 128
ROPE_THETA = 10000.0
ATTN_SCALE = HEAD_DIM ** -0.5
NEG_INF = -1e30
N_KEYS = 128
N_EXPERTS = N_KEYS * N_KEYS
PEER_HEADS = 8
PEER_TOPK = 16
D_QUERY = 256
HALF_Q = D_QUERY // 2
RMS_EPS = 1e-6
LN_EPS = 1e-5

LANES = 128
HEADS_PER_LANE_TILE = LANES // HEAD_DIM
N_HEAD_TILES = ATTN_WIDTH // LANES
NOT_SELECTED = 99.0
VMEM_LIMIT = 56 * 1024 * 1024


def _cparams(semantics, flags=None):
    return pltpu.CompilerParams(dimension_semantics=semantics, vmem_limit_bytes=VMEM_LIMIT, flags=flags)


def _rms(x, g):
    return (x * lax.rsqrt(jnp.mean(x * x, axis=-1, keepdims=True) + RMS_EPS)) * g


def _inproj_body(x_ref, g_ref, w_ref, cos_ref, sin_ref, q_ref, k_ref, v_ref, gl_ref):
    hb = _rms(x_ref[...], g_ref[...]).astype(BF16)
    cos = cos_ref[...]
    sin = sin_ref[...]
    lane = lax.broadcasted_iota(jnp.int32, cos.shape, 1)
    first_half = jnp.bitwise_and(lane, HEAD_DIM - 1) < HALF_HEAD

    def rope(t):
        partner = jnp.where(first_half, pltpu.roll(t, LANES - HALF_HEAD, 1), pltpu.roll(t, HALF_HEAD, 1))
        return t * cos + partner * sin

    q = jnp.dot(hb, w_ref[:, 0:ATTN_WIDTH], preferred_element_type=F32)
    k = jnp.dot(hb, w_ref[:, ATTN_WIDTH:2 * ATTN_WIDTH], preferred_element_type=F32)
    for c in range(N_HEAD_TILES):
        sl = slice(c * LANES, (c + 1) * LANES)
        q_ref[:, sl] = rope(q[:, sl]) * ATTN_SCALE
        k_ref[:, sl] = rope(k[:, sl])
    v_ref[...] = jnp.dot(hb, w_ref[:, 2 * ATTN_WIDTH:3 * ATTN_WIDTH], preferred_element_type=F32)
    u = jnp.dot(hb, w_ref[:, 3 * ATTN_WIDTH:IN_COLS], preferred_element_type=F32)
    gl_ref[...] = u[:, :CONV_CH] * jax.nn.sigmoid(u[:, CONV_CH:])


def _in_projection(x, g, w_bf16, cos, sin, tm):
    T = x.shape[0]
    assert cos.shape[0] % tm == 0 and T % cos.shape[0] == 0
    period = cos.shape[0] // tm
    tok = lambda i: (i, 0)
    pos = lambda i: (i % period, 0)
    full = lambda i: (0, 0)
    return pl.pallas_call(
        _inproj_body,
        grid=(T // tm,),
        in_specs=[pl.BlockSpec((tm, D_MODEL), tok), pl.BlockSpec((1, D_MODEL), full),
                  pl.BlockSpec((D_MODEL, IN_COLS), full),
                  pl.BlockSpec((tm, LANES), pos), pl.BlockSpec((tm, LANES), pos)],
        out_specs=[pl.BlockSpec((tm, ATTN_WIDTH), tok)] * 3 + [pl.BlockSpec((tm, CONV_CH), tok)],
        out_shape=[jax.ShapeDtypeStruct((T, ATTN_WIDTH), F32)] * 3 + [jax.ShapeDtypeStruct((T, CONV_CH), F32)],
        compiler_params=_cparams(("parallel",)),
        name="in_projection",
    )(x, g, w_bf16, cos, sin)


def _rope_tables(pos):
    inv = jnp.power(ROPE_THETA, -jnp.arange(HALF_HEAD, dtype=F32) * (2.0 / HEAD_DIM))
    ang = pos.astype(F32)[:, None] * inv[None, :]
    cos, sin = jnp.cos(ang), jnp.sin(ang)
    return (jnp.concatenate([cos, cos] * HEADS_PER_LANE_TILE, axis=-1),
            jnp.concatenate([-sin, sin] * HEADS_PER_LANE_TILE, axis=-1))


ATTN_MERGE_ROWS = 256


def _attn_prompt_body(q_ref, k_ref, v_ref, o_ref, ob_ref, mb_ref, lb_ref, *, seq):
    Q = BAND_BLOCK
    lane = lax.broadcasted_iota(jnp.int32, (Q, LANES), 1)
    head_lanes = [lane < HEAD_DIM, lane >= HEAD_DIM]
    q_minus_k = (lax.broadcasted_iota(jnp.int32, (Q, 2 * Q), 0)
                 - lax.broadcasted_iota(jnp.int32, (Q, 2 * Q), 1))

    for b, (win, dil) in enumerate(DILATIONS):
        reach = win // dil
        n_blocks = (seq // dil) // Q

        def rows(start, size, dil=dil):
            return pl.ds(start, size) if dil == 1 else pl.ds(start, size, stride=dil)

        def step(idx, carry, b=b, dil=dil, reach=reach, n_blocks=n_blocks, rows=rows):
            r = idx // n_blocks
            a0 = (idx % n_blocks) * Q
            ks = jnp.maximum(a0 - Q, 0)
            q_rows = rows(r + a0 * dil, Q)
            k_rows = rows(r + ks * dil, 2 * Q)
            q = q_ref[q_rows, :]
            k = k_ref[k_rows, :].astype(BF16)
            v = v_ref[k_rows, :].astype(BF16)
            off = a0 - ks
            valid = (q_minus_k >= -off) & (q_minus_k <= reach - off)
            o_full = jnp.zeros((Q, LANES), F32)
            m_full = jnp.zeros((Q, LANES), F32)
            l_full = jnp.zeros((Q, LANES), F32)
            for h in range(HEADS_PER_LANE_TILE):
                qh = jnp.where(head_lanes[h], q, 0.0).astype(BF16)
                s = lax.dot_general(qh, k, (((1,), (1,)), ((), ())), preferred_element_type=F32)
                s = jnp.where(valid, s, NEG_INF)
                m = jnp.max(s, axis=1, keepdims=True)
                p = jnp.exp(s - m)
                l = jnp.sum(p, axis=1, keepdims=True)
                pv = jnp.dot(p.astype(BF16), v, preferred_element_type=F32)
                o_full = jnp.where(head_lanes[h], pv, o_full)
                m_full = jnp.where(head_lanes[h], m, m_full)
                l_full = jnp.where(head_lanes[h], l, l_full)
            ob_ref[b, q_rows, :] = o_full
            mb_ref[b, q_rows, :] = m_full
            lb_ref[b, q_rows, :] = l_full
            return carry

        lax.fori_loop(0, dil * n_blocks, step, 0, unroll=8)

    def merge(i, carry):
        rws = pl.ds(pl.multiple_of(i * ATTN_MERGE_ROWS, ATTN_MERGE_ROWS), ATTN_MERGE_ROWS)
        n_br = len(DILATIONS)
        m_all = mb_ref[0, rws, :]
        for b in range(1, n_br):
            m_all = jnp.maximum(m_all, mb_ref[b, rws, :])
        num = jnp.zeros((ATTN_MERGE_ROWS, LANES), F32)
        den = jnp.zeros((ATTN_MERGE_ROWS, LANES), F32)
        for b in range(n_br):
            w = jnp.exp(mb_ref[b, rws, :] - m_all)
            num = num + w * ob_ref[b, rws, :]
            den = den + w * lb_ref[b, rws, :]
        o_ref[rws, :] = (num / den).astype(o_ref.dtype)
        return carry

    lax.fori_loop(0, seq // ATTN_MERGE_ROWS, merge, 0)


def _attention_prompt(q, k, v, batch, seq):
    T = q.shape[0]
    blk = pl.BlockSpec((seq, LANES), lambda b, hp: (b, hp))
    return pl.pallas_call(
        functools.partial(_attn_prompt_body, seq=seq),
        grid=(batch, N_HEAD_TILES),
        in_specs=[blk, blk, blk],
        out_specs=blk,
        out_shape=jax.ShapeDtypeStruct((T, ATTN_WIDTH), BF16),
        scratch_shapes=[pltpu.VMEM((len(DILATIONS), seq, LANES), F32)] * 3,
        compiler_params=_cparams(("parallel", "parallel")),
        name="attention_prompt",
    )(q, k, v)


SAMPLE_CHUNK = 128


def _attn_sample_body(q_ref, kn_ref, vn_ref, kt_ref, vt_ref, kg_ref, vg_ref, o_ref, s_ref, p_ref, m_ref, l_ref,
                      acc_ref, *, t_new, tail, past):
    c = pl.program_id(1)
    n_seq = q_ref.shape[-1]
    (w1, d1), (w4, d4), (w16, d16) = DILATIONS
    q = [q_ref[i] for i in range(t_new)]

    @pl.when(c == 0)
    def _():
        for i in range(t_new):
            s = [jnp.sum(q[i] * kn_ref[j], axis=0, keepdims=True) for j in range(i + 1)]
            m = s[0]
            for j in range(1, i + 1):
                m = jnp.maximum(m, s[j])
            l = jnp.zeros((1, n_seq), F32)
            acc = jnp.zeros((HEAD_DIM, n_seq), F32)
            for j in range(i + 1):
                p = (float(len(DILATIONS)) if j == i else 1.0) * jnp.exp(s[j] - m)
                l = l + p
                acc = acc + p * vn_ref[j]
            m_ref[i] = m
            l_ref[i] = l
            acc_ref[i] = acc

    def chunk(k_row_of, v_row_of):
        def score_row(r, carry):
            k_row = k_row_of(r)
            for i in range(t_new):
                s_ref[i, pl.ds(r, 1), :] = jnp.sum(q[i] * k_row, axis=0, keepdims=True)
            return carry

        lax.fori_loop(0, SAMPLE_CHUNK, score_row, 0, unroll=2)

        slot = c * SAMPLE_CHUNK + lax.broadcasted_iota(jnp.int32, (SAMPLE_CHUNK, n_seq), 0)
        in_tail = slot < tail
        cache_row = slot + (past - tail)
        grp = jnp.right_shift(slot - tail, shift)
        grp_row = jnp.bitwise_and(slot - tail, low)
        for i in range(t_new):
            in_d1 = in_tail & (cache_row >= past + i - w1 // d1)
            in_d4 = in_tail & (jnp.bitwise_and(cache_row - i, d4 - 1) == 0) & (cache_row >= past + i - w4)
            in_d16 = (~in_tail) & (grp_row == i) & (grp >= past // d16 - w16 // d16)
            mult = in_d1.astype(F32) + in_d4.astype(F32) + in_d16.astype(F32)
            s = jnp.where(mult > 0, s_ref[i], NEG_INF)
            m_old = m_ref[i]
            m_new = jnp.maximum(m_old, jnp.max(s, axis=0, keepdims=True))
            alpha = jnp.exp(m_old - m_new)
            p = mult * jnp.exp(s - m_new)
            p_ref[i] = p
            l_ref[i] = alpha * l_ref[i] + jnp.sum(p, axis=0, keepdims=True)
            m_ref[i] = m_new
            acc_ref[i] = acc_ref[i] * alpha

        def value_row(r, acc):
            v_row = v_row_of(r)
            return tuple(acc[i] + p_ref[i, pl.ds(r, 1), :] * v_row for i in range(t_new))

        acc = lax.fori_loop(0, SAMPLE_CHUNK, value_row, tuple(acc_ref[i] for i in range(t_new)), unroll=2)
        for i in range(t_new):
            acc_ref[i] = acc[i]

    shift, low = t_new.bit_length() - 1, t_new - 1
    n_groups = SAMPLE_CHUNK // t_new

    def strided_chunk(k_row_of, v_row_of, in_tail):
        def score_group(g, carry):
            for i in range(t_new):
                s_ref[i, pl.ds(g, 1), :] = jnp.sum(q[i] * k_row_of(g, i), axis=0, keepdims=True)
            return carry

        lax.fori_loop(0, n_groups, score_group, 0, unroll=2)

        grp_in_chunk = lax.broadcasted_iota(jnp.int32, (n_groups, n_seq), 0)
        for i in range(t_new):
            slot = c * SAMPLE_CHUNK + grp_in_chunk * t_new + i
            if in_tail:
                member = slot + (past - tail) >= past + i - w4
            else:
                member = jnp.right_shift(slot - tail, shift) >= past // d16 - w16 // d16
            s = jnp.where(member, s_ref[i, 0:n_groups, :], NEG_INF)
            m_old = m_ref[i]
            m_new = jnp.maximum(m_old, jnp.max(s, axis=0, keepdims=True))
            alpha = jnp.exp(m_old - m_new)
            p = jnp.where(member, jnp.exp(s - m_new), 0.0)
            p_ref[i, 0:n_groups, :] = p
            l_ref[i] = alpha * l_ref[i] + jnp.sum(p, axis=0, keepdims=True)
            m_ref[i] = m_new
            acc_ref[i] = acc_ref[i] * alpha

        def value_group(g, acc):
            return tuple(acc[i] + p_ref[i, pl.ds(g, 1), :] * v_row_of(g, i) for i in range(t_new))

        acc = lax.fori_loop(0, n_groups, value_group, tuple(acc_ref[i] for i in range(t_new)), unroll=2)
        for i in range(t_new):
            acc_ref[i] = acc[i]

    tail_chunks = tail // SAMPLE_CHUNK
    dense_chunks = -(-(w1 // d1) // SAMPLE_CHUNK)

    @pl.when(c < tail_chunks - dense_chunks)
    def _():
        strided_chunk(lambda g, i: kt_ref[g * t_new + i], lambda g, i: vt_ref[g * t_new + i], True)

    @pl.when((c >= tail_chunks - dense_chunks) & (c < tail_chunks))
    def _():
        chunk(lambda r: kt_ref[r], lambda r: vt_ref[r])

    @pl.when(c >= tail_chunks)
    def _():
        strided_chunk(lambda g, i: kg_ref[g, i], lambda g, i: vg_ref[g, i], False)

    @pl.when(c == pl.num_programs(1) - 1)
    def _():
        for i in range(t_new):
            o_ref[i] = acc_ref[i] / l_ref[i]


def _attention_sample(q, k_new, v_new, cache_k, cache_v):
    n_seq, t_new = q.shape[:2]
    past = cache_k.shape[1]
    (w1, d1), (w4, d4), (w16, d16) = DILATIONS
    tail = w4
    assert past % d16 == 0 and past >= w16 and w1 <= tail and t_new & (t_new - 1) == 0
    assert t_new == d4 and (past - tail) % d4 == 0 and d1 == 1
    tail_chunks = tail // SAMPLE_CHUNK
    group_chunks = (past // d16 * t_new) // SAMPLE_CHUNK
    groups_per_chunk = SAMPLE_CHUNK // t_new
    assert tail % SAMPLE_CHUNK == 0 and past % SAMPLE_CHUNK == 0 and (past // d16) % groups_per_chunk == 0
    lanes_last = lambda a: jnp.transpose(a, (1, 2, 3, 0))
    zero = jnp.right_shift((q[0, 0, 0, 0] == q[0, 0, 0, 0]).astype(jnp.int32), 1)
    tail_rows = lambda a: lanes_last(lax.dynamic_slice_in_dim(a, zero + (past - tail), tail, axis=1))
    group_rows = lambda a: jnp.transpose(lax.dynamic_slice_in_dim(
        a.reshape(n_seq, past // d16, d16, N_HEADS, HEAD_DIM), zero, t_new, axis=2), (1, 2, 3, 4, 0))
    new_blk = pl.BlockSpec((t_new, None, HEAD_DIM, n_seq), lambda h, c: (0, h, 0, 0))
    tail_blk = pl.BlockSpec((SAMPLE_CHUNK, None, HEAD_DIM, n_seq),
                            lambda h, c: (jnp.minimum(c, tail_chunks - 1), h, 0, 0))
    group_blk = pl.BlockSpec((groups_per_chunk, t_new, None, HEAD_DIM, n_seq),
                             lambda h, c: (jnp.maximum(c - tail_chunks, 0), 0, h, 0, 0))
    out = pl.pallas_call(
        functools.partial(_attn_sample_body, t_new=t_new, tail=tail, past=past),
        grid=(N_HEADS, tail_chunks + group_chunks),
        in_specs=[new_blk, new_blk, new_blk, tail_blk, tail_blk, group_blk, group_blk],
        out_specs=pl.BlockSpec((None, t_new, HEAD_DIM, n_seq), lambda h, c: (h, 0, 0, 0)),
        out_shape=jax.ShapeDtypeStruct((N_HEADS, t_new, HEAD_DIM, n_seq), F32),
        scratch_shapes=[pltpu.VMEM((t_new, SAMPLE_CHUNK, n_seq), F32), pltpu.VMEM((t_new, SAMPLE_CHUNK, n_seq), F32),
                        pltpu.VMEM((t_new, 1, n_seq), F32), pltpu.VMEM((t_new, 1, n_seq), F32),
                        pltpu.VMEM((t_new, HEAD_DIM, n_seq), F32)],
        compiler_params=_cparams(("parallel", "arbitrary")),
        name="attention_sample",
    )(lanes_last(q), lanes_last(k_new), lanes_last(v_new), tail_rows(cache_k), tail_rows(cache_v),
      group_rows(cache_k), group_rows(cache_v))
    return jnp.transpose(out, (3, 1, 0, 2)).reshape(n_seq, t_new, ATTN_WIDTH)


def _ln_swish(y, lg, lb):
    mu = jnp.mean(y, axis=-1, keepdims=True)
    var = jnp.mean(jnp.square(y - mu), axis=-1, keepdims=True)
    yn = (y - mu) * lax.rsqrt(var + LN_EPS) * lg + lb
    return yn * jax.nn.sigmoid(yn)


def _conv_prompt_body(cur_ref, prev_ref, cw_ref, cb_ref, lg_ref, lb_ref, o_ref, win_ref, *, ch):
    j = pl.program_id(1)
    win_ref[0:ch, :] = jnp.where(j == 0, 0.0, prev_ref[...])
    win_ref[ch:2 * ch, :] = cur_ref[...]
    acc = jnp.zeros((ch, CONV_CH), F32)
    for w in range(CONV_WIDTH):
        acc = acc + win_ref[ch - CONV_HIST + w:2 * ch - CONV_HIST + w, :] * cw_ref[w:w + 1, :]
    o_ref[...] = _ln_swish(acc + cb_ref[...], lg_ref[...], lb_ref[...]).astype(o_ref.dtype)


def _conv_prompt(g, cw, cb, lg, lb, batch, seq, ch):
    nj = seq // ch
    par = pl.BlockSpec((1, CONV_CH), lambda b, j: (0, 0))
    return pl.pallas_call(
        functools.partial(_conv_prompt_body, ch=ch),
        grid=(batch, nj),
        in_specs=[pl.BlockSpec((ch, CONV_CH), lambda b, j: (b * nj + j, 0)),
                  pl.BlockSpec((ch, CONV_CH), lambda b, j: (b * nj + jnp.maximum(j - 1, 0), 0)),
                  pl.BlockSpec((CONV_WIDTH, CONV_CH), lambda b, j: (0, 0)), par, par, par],
        out_specs=pl.BlockSpec((ch, CONV_CH), lambda b, j: (b * nj + j, 0)),
        out_shape=jax.ShapeDtypeStruct((batch * seq, CONV_CH), BF16),
        scratch_shapes=[pltpu.VMEM((2 * ch, CONV_CH), F32)],
        compiler_params=_cparams(("parallel", "parallel")),
        name="conv_prompt",
    )(g, g, cw, cb, lg, lb)


def _conv_sample_body(gh_ref, cw_ref, cb_ref, lg_ref, lb_ref, o_ref, *, t_new):
    for i in range(t_new):
        acc = jnp.zeros(gh_ref.shape[1:], F32)
        for w in range(CONV_WIDTH):
            acc = acc + gh_ref[i + w] * cw_ref[w:w + 1, :]
        o_ref[i] = _ln_swish(acc + cb_ref[...], lg_ref[...], lb_ref[...]).astype(o_ref.dtype)


def _conv_sample(gh_t, cw, cb, lg, lb):
    lh, n_seq, _ = gh_t.shape
    t_new = lh - CONV_HIST
    return pl.pallas_call(
        functools.partial(_conv_sample_body, t_new=t_new),
        out_shape=jax.ShapeDtypeStruct((t_new, n_seq, CONV_CH), BF16),
        name="conv_sample",
    )(gh_t, cw, cb, lg, lb)


def _outproj_body(attn_ref, c_ref, x_ref, wo_ref, g2_ref, wq_ref, keys_ref, x1_ref, h2t_ref, st_ref):
    a = jnp.dot(attn_ref[...], wo_ref[0:ATTN_WIDTH, :], preferred_element_type=F32)
    a = a + jnp.dot(c_ref[...], wo_ref[ATTN_WIDTH:D_MODEL, :], preferred_element_type=F32)
    x1 = x_ref[...] + a
    x1_ref[...] = x1
    h2 = _rms(x1, g2_ref[...])
    h2t_ref[...] = h2.T.astype(BF16)
    qv = jnp.dot(h2.astype(BF16), wq_ref[...], preferred_element_type=F32).astype(BF16)
    nt = (((1,), (1,)), ((), ()))
    for h in range(PEER_HEADS):
        for side in range(2):
            qs = qv[:, h * D_QUERY + side * HALF_Q:h * D_QUERY + (side + 1) * HALF_Q]
            st_ref[2 * h + side] = lax.dot_general(keys_ref[side], qs, nt, preferred_element_type=F32)


def _out_projection(attn, c, x, wo_bf16, g2, wq_bf16, keys_bf16, tm):
    T = x.shape[0]
    tok = lambda i: (i, 0)
    full2 = lambda i: (0, 0)
    return pl.pallas_call(
        _outproj_body,
        grid=(T // tm,),
        in_specs=[pl.BlockSpec((tm, ATTN_WIDTH), tok), pl.BlockSpec((tm, CONV_CH), tok),
                  pl.BlockSpec((tm, D_MODEL), tok), pl.BlockSpec((D_MODEL, D_MODEL), full2),
                  pl.BlockSpec((1, D_MODEL), full2), pl.BlockSpec((D_MODEL, PEER_HEADS * D_QUERY), full2),
                  pl.BlockSpec((2, N_KEYS, HALF_Q), lambda i: (0, 0, 0))],
        out_specs=[pl.BlockSpec((tm, D_MODEL), tok), pl.BlockSpec((D_MODEL, tm), lambda i: (0, i)),
                   pl.BlockSpec((2 * PEER_HEADS, N_KEYS, tm), lambda i: (0, 0, i))],
        out_shape=[jax.ShapeDtypeStruct((T, D_MODEL), F32), jax.ShapeDtypeStruct((D_MODEL, T), BF16),
                   jax.ShapeDtypeStruct((2 * PEER_HEADS, N_KEYS, T), F32)],
        compiler_params=_cparams(("parallel",)),
        name="out_projection",
    )(attn, c, x, wo_bf16, g2, wq_bf16, keys_bf16)


def _top16(s, order):
    rank = jnp.full(s.shape, NOT_SELECTED, F32)
    slot = lax.broadcasted_iota(jnp.int32, (PEER_TOPK, s.shape[1]), 0)
    vals = jnp.zeros((PEER_TOPK, s.shape[1]), F32)
    for a in range(PEER_TOPK):
        mx = jnp.max(s, axis=0, keepdims=True)
        first = jnp.min(jnp.where(s == mx, order, 1e9), axis=0, keepdims=True)
        sel = order == first
        rank = jnp.where(sel, float(a), rank)
        s = jnp.where(sel, -jnp.inf, s)
        vals = jnp.where(slot == a, mx, vals)
    return rank, vals


def _cex(a, b):
    if a is None or b is None:
        return (b if a is None else a), None
    return jnp.maximum(a, b), jnp.minimum(a, b)


def _bitonic_sort_desc(xs):
    xs, n, k = list(xs), len(xs), 2
    while k <= n:
        j = k // 2
        while j >= 1:
            for i in range(n):
                if i ^ j > i:
                    hi, lo = _cex(xs[i], xs[i ^ j])
                    xs[i], xs[i ^ j] = (hi, lo) if (i & k) == 0 else (lo, hi)
            j //= 2
        k *= 2
    return xs


def _bitonic_merge_desc(xs):
    xs, j = list(xs), len(xs) // 2
    while j >= 1:
        for i in range(len(xs)):
            if (i & j) == 0:
                xs[i], xs[i + j] = _cex(xs[i], xs[i + j])
        j //= 2
    return xs


def _top16_values(slabs):
    K = PEER_TOPK
    xs = _bitonic_sort_desc(list(slabs) + [None] * (K - len(slabs)))
    for shift in (4, 2, 1):
        other = [None if x is None else pltpu.roll(x, shift, 0) for x in xs]
        xs = _bitonic_merge_desc([_cex(xs[i], other[K - 1 - i])[0] for i in range(K)])
    return xs


def _search16(test, v):
    sel = jnp.where
    c1 = test(v[7])
    c2 = test(sel(c1, v[11], v[3]))
    c3 = test(sel(c1, sel(c2, v[13], v[9]), sel(c2, v[5], v[1])))
    c4 = test(sel(c1, sel(c2, sel(c3, v[14], v[12]), sel(c3, v[10], v[8])),
                  sel(c2, sel(c3, v[6], v[4]), sel(c3, v[2], v[0]))))
    return sel(c1, 8.0, 0.0) + sel(c2, 4.0, 0.0) + sel(c3, 2.0, 0.0) + sel(c4, 1.0, 0.0)


def _rows_sum(x):
    for shift in (4, 2, 1):
        x = x + pltpu.roll(x, shift, 0)
    return x


def _select_fast(s1, s2):
    K, R = PEER_TOPK, 8
    x1 = [s1[R * i:R * (i + 1)] for i in range(N_KEYS // R)]
    x2 = [s2[R * i:R * (i + 1)] for i in range(N_KEYS // R)]
    v1, v2 = _top16_values(x1), _top16_values(x2)
    row = lax.broadcasted_iota(jnp.int32, x1[0].shape, 0)

    def rows_of(vals):
        out = vals[0]
        for b in range(1, R):
            out = jnp.where(row == b, vals[b], out)
        return out

    v2_lo, v2_hi, v1_hi = rows_of(v2[0:R]), rows_of(v2[R:K]), rows_of(v1[R:K])
    cand = [v1[0] + v2_lo, v1[0] + v2_hi] + [v1[a] + v2_lo for a in range(1, R)] + [v1_hi + v2[0]]
    cs = _top16_values(cand)
    tau = cs[K - 1]
    z = jnp.exp(cs[0] - cs[0])
    for a in range(1, K):
        z = z + jnp.exp(cs[a] - cs[0])

    def count_ge(xs, t):
        n = jnp.zeros(xs[0].shape, F32)
        for x in xs:
            n = n + jnp.where(x >= t, 1.0, 0.0)
        return _rows_sum(n)

    tie = (count_ge(x1, v1[K - 1]) != K) | (count_ge(x2, v2[K - 1]) != K) | (count_ge(cand, tau) != K)
    for a in range(K - 1):
        tie = tie | (v1[a] == v1[a + 1]) | (v2[a] == v2[a + 1])

    p1, width, p2, r2 = [], [], [], []
    for x in x1:
        in1 = x >= v1[K - 1]
        wid = _search16(lambda vb, x=x: (x + vb) >= tau, v2) + jnp.where((x + v2[K - 1]) >= tau, 1.0, 0.0)
        p1.append(jnp.where(in1, jnp.exp(x - v1[0]) / z, 0.0))
        width.append(jnp.where(in1, wid, 0.0))
    for x in x2:
        in2 = x >= v2[K - 1]
        p2.append(jnp.where(in2, jnp.exp(x - v2[0]), 0.0))
        r2.append(jnp.where(in2, _search16(lambda vb, x=x: vb > x, v2), NOT_SELECTED))
    cat = lambda parts: jnp.concatenate(parts, axis=0)
    return cat(p1), cat(width), cat(p2), cat(r2), tie


def _select_exact(s1, s2, key_order, cand_order):
    K = PEER_TOPK
    half = K // 2
    slot16 = lax.broadcasted_iota(jnp.int32, (K, s1.shape[1]), 0)
    rank1, v1 = _top16(s1, key_order)
    rank2, v2 = _top16(s2, key_order)
    cand = [v1[0:1, :] + v2]
    cand += [v1[a:a + 1, :] + v2[0:half, :] for a in range(1, half)]
    cand += [v1[half:K, :] + v2[0:1, :]]
    cand = jnp.concatenate(cand, axis=0)
    crank, _ = _top16(cand, cand_order)
    chosen = crank < float(K)
    top = v1[0:1, :] + v2[0:1, :]
    z = jnp.sum(jnp.where(chosen, jnp.exp(cand - top), 0.0), axis=0, keepdims=True)
    chosen_f = chosen.astype(F32)
    count = jnp.zeros((K, s1.shape[1]), F32)
    count = jnp.where(slot16 == 0, jnp.sum(chosen_f[0:K], axis=0, keepdims=True), count)
    for a in range(1, half):
        lo = K + (a - 1) * half
        count = jnp.where(slot16 == a, jnp.sum(chosen_f[lo:lo + half], axis=0, keepdims=True), count)
    count = jnp.concatenate([count[0:half], chosen_f[K + (half - 1) * half:]], axis=0)
    width = jnp.zeros(s1.shape, F32)
    for a in range(K):
        width = jnp.where(rank1 == float(a), count[a:a + 1, :], width)
    in1 = rank1 < float(K)
    in2 = rank2 < float(K)
    return (jnp.where(in1, jnp.exp(s1 - v1[0:1, :]) / z, 0.0), width,
            jnp.where(in2, jnp.exp(s2 - v2[0:1, :]), 0.0), rank2)


def _peer_select_body(st_ref, p1_ref, c1_ref, p2_ref, r2_ref, *, tl):
    K = PEER_TOPK
    half = K // 2
    key_order = lax.broadcasted_iota(jnp.int32, (N_KEYS, LANES), 0).astype(F32)
    sub = lax.broadcasted_iota(jnp.int32, (half, LANES), 0)
    flat = [lax.broadcasted_iota(jnp.int32, (K, LANES), 0)]
    flat += [a * K + sub for a in range(1, half)]
    flat += [(half + sub) * K]
    cand_order = jnp.concatenate(flat, axis=0).astype(F32)

    def one(idx, carry):
        h = idx // (tl // LANES)
        ls = pl.multiple_of((idx % (tl // LANES)) * LANES, LANES)
        lanes = pl.ds(ls, LANES)
        s1 = st_ref[2 * h, :, lanes]
        s2 = st_ref[2 * h + 1, :, lanes]

        def write(p1, width, p2, r2):
            p1_ref[h, :, lanes] = p1
            c1_ref[h, :, lanes] = width
            p2_ref[h, :, lanes] = p2.astype(p2_ref.dtype)
            r2_ref[h, :, lanes] = r2.astype(r2_ref.dtype)

        *fast, tie = _select_fast(s1, s2)
        write(*fast)

        @pl.when(jnp.max(jnp.where(tie, 1.0, 0.0)) > 0.0)
        def _():
            write(*_select_exact(s1, s2, key_order, cand_order))

        return carry

    lax.fori_loop(0, PEER_HEADS * (tl // LANES), one, 0)


def _peer_select(st, tl):
    T = st.shape[2]
    out_blk = pl.BlockSpec((PEER_HEADS, N_KEYS, tl), lambda i: (0, 0, i))
    shp = jax.ShapeDtypeStruct((PEER_HEADS, N_KEYS, T), F32)
    return pl.pallas_call(
        functools.partial(_peer_select_body, tl=tl),
        grid=(T // tl,),
        in_specs=[pl.BlockSpec((2 * PEER_HEADS, N_KEYS, tl), lambda i: (0, 0, i))],
        out_specs=[out_blk] * 4,
        out_shape=[shp, shp, jax.ShapeDtypeStruct(shp.shape, BF16), jax.ShapeDtypeStruct(shp.shape, BF16)],
        compiler_params=_cparams(("parallel",)),
        name="peer_select",
    )(st)


def _peer_body(h2t_ref, p1_ref, c1_ref, p2_in_ref, r2_in_ref, u_ref, vt_ref, o_ref, a_ref, wg_ref, p2_ref, r2_ref,
               *, te):
    e = pl.program_id(1)

    @pl.when(e == 0)
    def _():
        o_ref[...] = jnp.zeros(o_ref.shape, F32)
        p2_ref[...] = p2_in_ref[...]
        r2_ref[...] = r2_in_ref[...]

    tm = o_ref.shape[1]
    a_ref[...] = jnp.dot(u_ref[...], h2t_ref[...], preferred_element_type=F32)
    for j in range(te // N_KEYS):
        rows = slice(j * N_KEYS, (j + 1) * N_KEYS)
        for lt in range(tm // LANES):
            lanes = slice(lt * LANES, (lt + 1) * LANES)
            tile = (N_KEYS, LANES)
            w = jnp.zeros(tile, BF16)
            for h in range(PEER_HEADS):
                width = jnp.broadcast_to(c1_ref[h, j:j + 1, lanes], tile).astype(BF16)
                pfirst = jnp.broadcast_to(p1_ref[h, j:j + 1, lanes], tile).astype(BF16)
                keep = r2_ref[h, :, lanes] < width
                w = w + jnp.where(keep, p2_ref[h, :, lanes], jnp.zeros(tile, BF16)) * pfirst
            act = jax.nn.gelu(a_ref[rows, lanes], approximate=True).astype(BF16)
            wg_ref[rows, lanes] = w * act
    o_ref[...] += jnp.dot(vt_ref[...], wg_ref[...], preferred_element_type=F32)


def _peer_experts(h2t, p1, c1, p2, r2, u_bf16, vt_bf16, tm, te):
    T = h2t.shape[1]
    firsts = te // N_KEYS
    sel_all = pl.BlockSpec((PEER_HEADS, N_KEYS, tm), lambda i, e: (0, 0, i))
    sel_blk = pl.BlockSpec((PEER_HEADS, firsts, tm), lambda i, e: (0, e, i))
    return pl.pallas_call(
        functools.partial(_peer_body, te=te),
        grid=(T // tm, N_EXPERTS // te),
        in_specs=[pl.BlockSpec((D_MODEL, tm), lambda i, e: (0, i)), sel_blk, sel_blk, sel_all, sel_all,
                  pl.BlockSpec((te, D_MODEL), lambda i, e: (e, 0)),
                  pl.BlockSpec((D_MODEL, te), lambda i, e: (0, e))],
        out_specs=pl.BlockSpec((D_MODEL, tm), lambda i, e: (0, i)),
        out_shape=jax.ShapeDtypeStruct((D_MODEL, T), F32),
        scratch_shapes=[pltpu.VMEM((te, tm), F32), pltpu.VMEM((te, tm), BF16),
                        pltpu.VMEM((PEER_HEADS, N_KEYS, tm), BF16), pltpu.VMEM((PEER_HEADS, N_KEYS, tm), BF16)],
        compiler_params=_cparams(("parallel", "arbitrary")),
        name="peer_experts",
    )(h2t, p1, c1, p2, r2, u_bf16, vt_bf16)


def _final_body(x1_ref, ot_ref, g_ref, y_ref):
    y_ref[...] = _rms(x1_ref[...] + ot_ref[...].T, g_ref[...])


def _final_norm(x1, out_t, g, tm):
    T = x1.shape[0]
    return pl.pallas_call(
        _final_body,
        grid=(T // tm,),
        in_specs=[pl.BlockSpec((tm, D_MODEL), lambda i: (i, 0)), pl.BlockSpec((D_MODEL, tm), lambda i: (0, i)),
                  pl.BlockSpec((1, D_MODEL), lambda i: (0, 0))],
        out_specs=pl.BlockSpec((tm, D_MODEL), lambda i: (i, 0)),
        out_shape=jax.ShapeDtypeStruct((T, D_MODEL), F32),
        compiler_params=_cparams(("parallel",)),
        name="final_norm",
    )(x1, out_t, g)


TOKEN_TILE = 512
PEER_TOKEN_TILE = 512
PEER_EXPERT_TILE = 2048
CONV_ROWS = 512


def _channel_and_final(attn, c, x, p):
    x1, h2t, st = _out_projection(attn, c, x, p["wo"], p["g2"], p["wq"], p["keys"], TOKEN_TILE)
    p1, c1, p2, r2 = _peer_select(st, TOKEN_TILE)
    out_t = _peer_experts(h2t, p1, c1, p2, r2, p["u"], p["vt"], min(PEER_TOKEN_TILE, x.shape[0]), PEER_EXPERT_TILE)
    return _final_norm(x1, out_t, p["gf"], TOKEN_TILE)


def kernel(x_prompt, x_sample, cache_k, cache_v, state_conv, norm1_g, w_in, conv_w, conv_b, conv_ln_g,
           conv_ln_b, w_out, norm2_g, w_query, sub_keys1, sub_keys2, expert_u, expert_v, final_g):
    bp, s, _ = x_prompt.shape
    bs, t, _ = x_sample.shape
    depth, _, w_buf = cache_k.shape[:3]
    assert depth == 1 and all(s % (d * BAND_BLOCK) == 0 and s // d >= 2 * BAND_BLOCK for _, d in DILATIONS)
    keep = min(DILATIONS[-1][0], s)
    l = 0
    row = lambda a: a.reshape(1, -1)
    p = dict(
        wo=w_out[l].astype(BF16), g2=row(norm2_g[l]), wq=w_query[l].astype(BF16),
        keys=jnp.stack([sub_keys1[l], sub_keys2[l]]).astype(BF16),
        u=expert_u[l].astype(BF16), vt=expert_v[l].T.astype(BF16), gf=row(final_g),
    )
    w_in_b = w_in[l].astype(BF16)
    g1 = row(norm1_g[l])
    cw, cb, lg, lb = conv_w[l], row(conv_b[l]), row(conv_ln_g[l]), row(conv_ln_b[l])

    xp = x_prompt.reshape(bp * s, D_MODEL)
    cos, sin = _rope_tables(jnp.arange(s))
    q, k, v, g = _in_projection(xp, g1, w_in_b, cos, sin, TOKEN_TILE)
    attn = _attention_prompt(q, k, v, bp, s)
    c = _conv_prompt(g, cw, cb, lg, lb, bp, s, CONV_ROWS)
    y_prompt = _channel_and_final(attn, c, xp, p).reshape(bp, s, D_MODEL)
    kept = lambda a: a.reshape(bp, s, ATTN_WIDTH)[:, s - keep:].reshape(1, bp, keep, N_HEADS, HEAD_DIM)
    new_k_prompt, new_v_prompt = kept(k), kept(v)
    new_conv_prompt = g.reshape(bp, s, CONV_CH)[None, :, s - CONV_HIST:]

    xs = x_sample.reshape(bs * t, D_MODEL)
    assert w_buf == min(MAX_WINDOW, PAST_LEN)
    cos, sin = _rope_tables(jnp.tile(PAST_LEN + jnp.arange(t), bs))
    q, k, v, g = _in_projection(xs, g1, w_in_b, cos, sin, TOKEN_TILE)
    heads = lambda a: a.reshape(bs, t, N_HEADS, HEAD_DIM)
    attn = _attention_sample(heads(q), heads(k), heads(v), cache_k[l], cache_v[l])
    attn = attn.reshape(bs * t, ATTN_WIDTH).astype(BF16)
    g_hist = jnp.concatenate([state_conv[l], g.reshape(bs, t, CONV_CH)], axis=1)
    c = _conv_sample(jnp.transpose(g_hist, (1, 0, 2)), cw, cb, lg, lb)
    c = jnp.transpose(c, (1, 0, 2)).reshape(bs * t, CONV_CH)
    y_sample = _channel_and_final(attn, c, xs, p).reshape(bs, t, D_MODEL)
    new_k_sample = k.reshape(bs, t, N_HEADS, HEAD_DIM)[None]
    new_v_sample = v.reshape(bs, t, N_HEADS, HEAD_DIM)[None]
    new_conv_sample = g_hist[None, :, -CONV_HIST:]

    return (y_prompt, y_sample, new_k_prompt, new_v_prompt, new_conv_prompt,
            new_k_sample, new_v_sample, new_conv_sample)
```
